```python
import math
import jax
import jax.numpy as jnp
from jax import lax

D_MODEL = 1024
BATCH = 4
SEQ = 4096
DEPTH = 4

N_MIXERS = 4
D_FF = 2816
NORM_EPS = 1e-6
CONV_K = 4

GDN_HEADS = 8
GDN_DK = 128
GDN_DV = 128
GDN_CHUNK = 64
GDN_QKV = 2 * GDN_HEADS * GDN_DK + GDN_HEADS * GDN_DV
GDN_PROJ = GDN_QKV + GDN_HEADS * GDN_DV + 2 * GDN_HEADS

RWKV_HEAD = 64
RWKV_HEADS = D_MODEL // RWKV_HEAD
RWKV_DECAY_LORA = 64
RWKV_AAA_LORA = 64
RWKV_GATE_LORA = 128
RWKV_GN_EPS = 64e-5

SSD_INNER = 2 * D_MODEL
SSD_HEADDIM = 64
SSD_HEADS = SSD_INNER // SSD_HEADDIM
SSD_GROUPS = 4
SSD_STATE = 128
SSD_CHUNK = 128
SSD_XBC = SSD_INNER + 2 * SSD_GROUPS * SSD_STATE
SSD_PROJ = SSD_INNER + SSD_XBC + SSD_HEADS
SSD_NORM_EPS = 1e-5

LRU_WIDTH = D_MODEL
LRU_BLOCKS = 4
LRU_BW = LRU_WIDTH // LRU_BLOCKS
LRU_C = 8.0

kernel_name = 'hybrid_macaron_gdn_rwkv7_ssd_rglru'


def _n_of_type(m):
    return (DEPTH - m + N_MIXERS - 1) // N_MIXERS


def rmsnorm(x, g, eps=NORM_EPS):
    xf = x.astype(jnp.float32)
    y = xf * lax.rsqrt(jnp.mean(xf * xf, axis=-1, keepdims=True) + eps)
    return (y * g.astype(jnp.float32)).astype(x.dtype)


def l2norm(x, eps=1e-6):
    return x * lax.rsqrt(jnp.sum(x * x, axis=-1, keepdims=True) + eps)


def swiglu(x, w_in, w_out):
    gate, up = jnp.split(x @ w_in, 2, axis=-1)
    return (jax.nn.silu(gate) * up) @ w_out


def causal_dwconv(x, w):
    taps, T = w.shape[0], x.shape[1]
    xp = jnp.pad(x, ((0, 0), (taps - 1, 0), (0, 0)))
    return sum(xp[:, j:j + T] * w[j] for j in range(taps))


def segsum(a):
    n = a.shape[-1]
    cs = jnp.cumsum(a, axis=-1)
    mask = jnp.tril(jnp.ones((n, n), dtype=bool))
    return jnp.where(mask, cs[..., :, None] - cs[..., None, :], -jnp.inf)


def gated_deltanet(x, w_in, conv_w, a_log, dt_bias, norm_w, w_out):
    Bsz, T, _ = x.shape
    H, DK, DV, C = GDN_HEADS, GDN_DK, GDN_DV, GDN_CHUNK
    f32 = jnp.float32
    N = T // C
    proj = x @ w_in
    qkv, z, b, a = jnp.split(proj, [GDN_QKV, GDN_QKV + H * DV, GDN_QKV + H * DV + H], axis=-1)
    qkv = jax.nn.silu(causal_dwconv(qkv, conv_w))
    q, k, v = jnp.split(qkv, [H * DK, 2 * H * DK], axis=-1)
    q = l2norm(q.reshape(Bsz, T, H, DK).astype(f32)) * DK ** -0.5
    k = l2norm(k.reshape(Bsz, T, H, DK).astype(f32))
    v = v.reshape(Bsz, T, H, DV).astype(f32)
    beta = jax.nn.sigmoid(b.astype(f32))
    g = -jnp.exp(a_log.astype(f32)) * jax.nn.softplus(a.astype(f32) + dt_bias.astype(f32))

    def chunk(t):
        return t.reshape(Bsz, N, C, H, -1).transpose(0, 3, 1, 2, 4)

    q, k, v = chunk(q), chunk(k), chunk(v)
    beta = beta.reshape(Bsz, N, C, H).transpose(0, 3, 1, 2)
    g = jnp.cumsum(g.reshape(Bsz, N, C, H).transpose(0, 3, 1, 2), axis=-1)
    causal = jnp.tril(jnp.ones((C, C), dtype=bool))
    strict = jnp.tril(jnp.ones((C, C), dtype=bool), -1)
    decay = jnp.exp(jnp.where(causal, g[..., :, None] - g[..., None, :], -jnp.inf))
    kk = jnp.einsum('bhncd,bhnsd->bhncs', k, k)
    lower = jnp.where(strict, kk * decay * beta[..., None], 0.0)
    eye = jnp.eye(C, dtype=f32)
    rhs = jnp.concatenate([v * beta[..., None], k * (beta * jnp.exp(g))[..., None]], axis=-1)
    sol = lax.linalg.triangular_solve(eye + lower, rhs, left_side=True, lower=True, unit_diagonal=True)
    u, w = sol[..., :DV], sol[..., DV:]
    qk = jnp.einsum('bhncd,bhnsd->bhncs', q, k) * decay
    q_dec = q * jnp.exp(g)[..., None]
    k_dec = k * jnp.exp(g[..., -1:] - g)[..., None]
    g_last = jnp.exp(g[..., -1])

    def step(S, inp):
        u_i, w_i, qk_i, qd_i, kd_i, gl_i = inp
        v_new = u_i - jnp.einsum('bhck,bhkv->bhcv', w_i, S)
        o = jnp.einsum('bhck,bhkv->bhcv', qd_i, S) + jnp.einsum('bhcs,bhsv->bhcv', qk_i, v_new)
        S = S * gl_i[..., None, None] + jnp.einsum('bhck,bhcv->bhkv', kd_i, v_new)
        return S, o

    xs = tuple(jnp.moveaxis(t, 2, 0) for t in (u, w, qk, q_dec, k_dec, g_last))
    S0 = jnp.zeros((Bsz, H, DK, DV), f32)
    _, o = lax.scan(step, S0, xs)
    o = o.transpose(1, 0, 3, 2, 4).reshape(Bsz, T, H, DV)
    o = rmsnorm(o, norm_w) * jax.nn.silu(z.reshape(Bsz, T, H, DV).astype(f32))
    return o.reshape(Bsz, T, H * DV).astype(x.dtype) @ w_out


def rwkv7_time_mix(x, mu, w_rkv, w0, w1, w2, a0, a1, a2, g1, g2, k_k, k_a, r_k, ln_w, ln_b, w_out):
    Bsz, T, D = x.shape
    H, N = RWKV_HEADS, RWKV_HEAD
    f32 = jnp.float32
    xx = jnp.pad(x, ((0, 0), (1, 0), (0, 0)))[:, :-1] - x
    xm = x[None] + xx[None] * mu[:, None, None, :]
    rkv = jnp.einsum('nbtd,nde->nbte', xm[:3], w_rkv).astype(f32)
    r, k, v = rkv[0], rkv[1], rkv[2]
    w_log = -jax.nn.softplus(-(w0 + jnp.tanh(xm[3] @ w1) @ w2).astype(f32)) - 0.5
    decay = jnp.exp(-jnp.exp(w_log))
    a = jax.nn.sigmoid((a0 + (xm[4] @ a1) @ a2).astype(f32))
    g = (jax.nn.sigmoid(xm[5] @ g1) @ g2).astype(f32)

    def heads(t):
        return t.reshape(Bsz, T, H, N)

    kk = l2norm(heads(k * k_k))
    k = k * (1.0 + (a - 1.0) * k_a)
    rh, kh, vh = heads(r), heads(k), heads(v)
    seq = tuple(jnp.moveaxis(t, 1, 0) for t in (rh, heads(decay), kh, vh, -kk, kk * heads(a)))

    def step(S, inp):
        r_t, w_t, k_t, v_t, a_t, b_t = inp
        sa = jnp.einsum('bhvk,bhk->bhv', S, a_t)
        S = S * w_t[:, :, None, :] + sa[..., None] * b_t[:, :, None, :] + v_t[..., None] * k_t[:, :, None, :]
        return S, jnp.einsum('bhvk,bhk->bhv', S, r_t)

    S0 = jnp.zeros((Bsz, H, N, N), f32)
    _, y = lax.scan(step, S0, seq)
    y = jnp.moveaxis(y, 0, 1)
    mean = jnp.mean(y, axis=-1, keepdims=True)
    var = jnp.mean(jnp.square(y - mean), axis=-1, keepdims=True)
    y = ((y - mean) * lax.rsqrt(var + RWKV_GN_EPS)).reshape(Bsz, T, D) * ln_w + ln_b
    bonus = jnp.sum(rh * kh * r_k, axis=-1, keepdims=True) * vh
    y = (y + bonus.reshape(Bsz, T, D)) * g
    return y.astype(x.dtype) @ w_out


def mamba2_ssd(x, w_in, conv_w, conv_b, dt_bias, a_log, d_skip, norm_w, w_out):
    Bsz, T, _ = x.shape
    G, NS, P, C = SSD_GROUPS, SSD_STATE, SSD_HEADDIM, SSD_CHUNK
    E = SSD_HEADS // G
    NC = T // C
    f32 = jnp.float32
    proj = x @ w_in
    z, xbc, dt = jnp.split(proj, [SSD_INNER, SSD_INNER + SSD_XBC], axis=-1)
    xbc = jax.nn.silu(causal_dwconv(xbc, conv_w) + conv_b)
    xs, Bm, Cm = jnp.split(xbc, [SSD_INNER, SSD_INNER + G * NS], axis=-1)
    dt = jax.nn.softplus(dt.astype(f32) + dt_bias.astype(f32))
    A = -jnp.exp(a_log.astype(f32))
    X = xs.astype(f32).reshape(Bsz, NC, C, G, E, P)
    Bc = Bm.astype(f32).reshape(Bsz, NC, C, G, NS)
    Cc = Cm.astype(f32).reshape(Bsz, NC, C, G, NS)
    Xdt = X * dt.reshape(Bsz, NC, C, G, E)[..., None]
    dA = (dt * A).reshape(Bsz, NC, C, G, E).transpose(0, 3, 4, 1, 2)
    A_cs = jnp.cumsum(dA, axis=-1)
    CB = jnp.einsum('bclgn,bcsgn->bgcls', Cc, Bc)
    scores = CB[:, :, None] * jnp.exp(segsum(dA))
    Y_diag = jnp.einsum('bgecls,bcsgep->bclgep', scores, Xdt)
    decay_states = jnp.exp(A_cs[..., -1:] - A_cs)
    states = jnp.einsum('bclgn,bgecl,bclgep->bcgepn', Bc, decay_states, Xdt)
    states = jnp.concatenate([jnp.zeros_like(states[:, :1]), states], axis=1)
    chunk_tot = jnp.pad(A_cs[..., -1], ((0, 0), (0, 0), (0, 0), (1, 0)))
    decay_chunk = jnp.exp(segsum(chunk_tot))
    prev_states = jnp.einsum('bgezc,bcgepn->bzgepn', decay_chunk, states)[:, :-1]
    Y_off = jnp.einsum('bclgn,bcgepn,bgecl->bclgep', Cc, prev_states, jnp.exp(A_cs))
    y = (Y_diag + Y_off + X * d_skip.astype(f32).reshape(G, E, 1)).reshape(Bsz, T, SSD_INNER)
    y = y * jax.nn.silu(z.astype(f32))
    y = rmsnorm(y.reshape(Bsz, T, G, SSD_INNER // G), norm_w.reshape(G, SSD_INNER // G), SSD_NORM_EPS)
    return y.reshape(Bsz, T, SSD_INNER).astype(x.dtype) @ w_out


def rglru_block(x, w_in, conv_w, conv_b, w_gates, b_gates, lam, w_out):
    Bsz, T, _ = x.shape
    f32 = jnp.float32
    gate, xb = jnp.split(x @ w_in, 2, axis=-1)
    u = (causal_dwconv(xb, conv_w) + conv_b).astype(f32)
    gates = jnp.einsum('btnc,ncd->btnd', u.reshape(Bsz, T, LRU_BLOCKS, LRU_BW), w_gates.astype(f32))
    gates = gates.reshape(Bsz, T, LRU_BLOCKS, 2, LRU_BW)
    r = jax.nn.sigmoid(gates[:, :, :, 0].reshape(Bsz, T, LRU_WIDTH) + b_gates[0])
    i = jax.nn.sigmoid(gates[:, :, :, 1].reshape(Bsz, T, LRU_WIDTH) + b_gates[1])
    log_a = -LRU_C * r * jax.nn.softplus(-lam.astype(f32))
    a = jnp.exp(log_a)
    bterm = jnp.sqrt(-jnp.expm1(2.0 * log_a)) * (i * u)

    def combine(p, q):
        a1, b1 = p
        a2, b2 = q
        return a1 * a2, a2 * b1 + b2

    _, h = lax.associative_scan(combine, (a, bterm), axis=1)
    y = jax.nn.gelu(gate.astype(f32)) * h
    return y.astype(x.dtype) @ w_out


def setup_inputs(seed: int = 0) -> dict:
    key = jax.random.key(seed)
    ks = iter(list(jax.random.split(key, 48)))
    f32 = jnp.float32

    def nrm(shape, scale):
        return scale * jax.random.normal(next(ks), shape, f32)

    def unif(shape, lo, hi):
        return jax.random.uniform(next(ks), shape, f32, lo, hi)

    def dt_bias(shape):
        dt = jnp.exp(unif(shape, math.log(1e-3), math.log(1e-1)))
        return dt + jnp.log(-jnp.expm1(-dt))

    nA, nB, nC, nD = (_n_of_type(m) for m in range(N_MIXERS))
    D = D_MODEL
    x = nrm((BATCH, SEQ, D), 1.0)
    norm_g = 1.0 + nrm((DEPTH, 6, D), 0.02)
    ffn_w_in = nrm((DEPTH, 2, D, 2 * D_FF), D ** -0.5)
    ffn_w_out = nrm((DEPTH, 2, D_FF, D), D_FF ** -0.5)
    gdn_w_in = nrm((nA, D, GDN_PROJ), D ** -0.5)
    gdn_conv_w = nrm((nA, CONV_K, GDN_QKV), CONV_K ** -0.5)
    gdn_a_log = jnp.log(unif((nA, GDN_HEADS), 1.0, 16.0))
    gdn_dt_bias = dt_bias((nA, GDN_HEADS))
    gdn_norm_w = 1.0 + nrm((nA, GDN_DV), 0.02)
    gdn_w_out = nrm((nA, GDN_HEADS * GDN_DV, D), (GDN_HEADS * GDN_DV) ** -0.5)
    rwkv_mu = unif((nB, 6, D), 0.0, 1.0)
    rwkv_w_rkv = nrm((nB, 3, D, D), D ** -0.5)
    rwkv_w0 = unif((nB, D), -6.5, -1.5)
    rwkv_w1 = nrm((nB, D, RWKV_DECAY_LORA), D ** -0.5)
    rwkv_w2 = nrm((nB, RWKV_DECAY_LORA, D), 0.1 * RWKV_DECAY_LORA ** -0.5)
    rwkv_a0 = nrm((nB, D), 0.1)
    rwkv_a1 = nrm((nB, D, RWKV_AAA_LORA), D ** -0.5)
    rwkv_a2 = nrm((nB, RWKV_AAA_LORA, D), 0.1 * RWKV_AAA_LORA ** -0.5)
    rwkv_g1 = nrm((nB, D, RWKV_GATE_LORA), D ** -0.5)
    rwkv_g2 = nrm((nB, RWKV_GATE_LORA, D), RWKV_GATE_LORA ** -0.5)
    rwkv_k_k = 0.85 + nrm((nB, D), 0.02)
    rwkv_k_a = 1.0 + nrm((nB, D), 0.02)
    rwkv_r_k = nrm((nB, RWKV_HEADS, RWKV_HEAD), 0.1)
    rwkv_ln_w = 1.0 + nrm((nB, D), 0.02)
    rwkv_ln_b = nrm((nB, D), 0.02)
    rwkv_w_out = nrm((nB, D, D), D ** -0.5)
    ssd_w_in = nrm((nC, D, SSD_PROJ), D ** -0.5)
    ssd_conv_w = nrm((nC, CONV_K, SSD_XBC), CONV_K ** -0.5)
    ssd_conv_b = nrm((nC, SSD_XBC), 0.02)
    ssd_dt_bias = dt_bias((nC, SSD_HEADS))
    ssd_a_log = jnp.log(unif((nC, SSD_HEADS), 1.0, 16.0))
    ssd_d = 1.0 + nrm((nC, SSD_HEADS), 0.1)
    ssd_norm_w = 1.0 + nrm((nC, SSD_INNER), 0.02)
    ssd_w_out = nrm((nC, SSD_INNER, D), SSD_INNER ** -0.5)
    lru_w_in = nrm((nD, D, 2 * LRU_WIDTH), D ** -0.5)
    lru_conv_w = nrm((nD, CONV_K, LRU_WIDTH), CONV_K ** -0.5)
    lru_conv_b = nrm((nD, LRU_WIDTH), 0.02)
    lru_w_gates = nrm((nD, LRU_BLOCKS, LRU_BW, 2 * LRU_BW), LRU_BW ** -0.5)
    lru_b_gates = nrm((nD, 2, LRU_WIDTH), 0.02)
    s = unif((nD, LRU_WIDTH), 0.9, 0.999) ** (1.0 / LRU_C)
    lru_lam = jnp.log(s) - jnp.log1p(-s)
    lru_w_out = nrm((nD, LRU_WIDTH, D), LRU_WIDTH ** -0.5)
    return {
        'x': x, 'norm_g': norm_g, 'ffn_w_in': ffn_w_in, 'ffn_w_out': ffn_w_out,
        'gdn_w_in': gdn_w_in, 'gdn_conv_w': gdn_conv_w, 'gdn_a_log': gdn_a_log,
        'gdn_dt_bias': gdn_dt_bias, 'gdn_norm_w': gdn_norm_w, 'gdn_w_out': gdn_w_out,
        'rwkv_mu': rwkv_mu, 'rwkv_w_rkv': rwkv_w_rkv, 'rwkv_w0': rwkv_w0, 'rwkv_w1': rwkv_w1,
        'rwkv_w2': rwkv_w2, 'rwkv_a0': rwkv_a0, 'rwkv_a1': rwkv_a1, 'rwkv_a2': rwkv_a2,
        'rwkv_g1': rwkv_g1, 'rwkv_g2': rwkv_g2, 'rwkv_k_k': rwkv_k_k, 'rwkv_k_a': rwkv_k_a,
        'rwkv_r_k': rwkv_r_k, 'rwkv_ln_w': rwkv_ln_w, 'rwkv_ln_b': rwkv_ln_b, 'rwkv_w_out': rwkv_w_out,
        'ssd_w_in': ssd_w_in, 'ssd_conv_w': ssd_conv_w, 'ssd_conv_b': ssd_conv_b,
        'ssd_dt_bias': ssd_dt_bias, 'ssd_a_log': ssd_a_log, 'ssd_d': ssd_d,
        'ssd_norm_w': ssd_norm_w, 'ssd_w_out': ssd_w_out,
        'lru_w_in': lru_w_in, 'lru_conv_w': lru_conv_w, 'lru_conv_b': lru_conv_b,
        'lru_w_gates': lru_w_gates, 'lru_b_gates': lru_b_gates, 'lru_lam': lru_lam,
        'lru_w_out': lru_w_out,
    }


def reference(x, norm_g, ffn_w_in, ffn_w_out,
              gdn_w_in, gdn_conv_w, gdn_a_log, gdn_dt_bias, gdn_norm_w, gdn_w_out,
              rwkv_mu, rwkv_w_rkv, rwkv_w0, rwkv_w1, rwkv_w2, rwkv_a0, rwkv_a1, rwkv_a2,
              rwkv_g1, rwkv_g2, rwkv_k_k, rwkv_k_a, rwkv_r_k, rwkv_ln_w, rwkv_ln_b, rwkv_w_out,
              ssd_w_in, ssd_conv_w, ssd_conv_b, ssd_dt_bias, ssd_a_log, ssd_d, ssd_norm_w, ssd_w_out,
              lru_w_in, lru_conv_w, lru_conv_b, lru_w_gates, lru_b_gates, lru_lam, lru_w_out):
    for i in range(DEPTH):
        m, j = i % N_MIXERS, i // N_MIXERS
        g = norm_g[i]
        x = x + 0.5 * rmsnorm(swiglu(rmsnorm(x, g[0]), ffn_w_in[i, 0], ffn_w_out[i, 0]), g[1])
        h = rmsnorm(x, g[2])
        if m == 0:
            h = gated_deltanet(h, gdn_w_in[j], gdn_conv_w[j], gdn_a_log[j], gdn_dt_bias[j],
                               gdn_norm_w[j], gdn_w_out[j])
        elif m == 1:
            h = rwkv7_time_mix(h, rwkv_mu[j], rwkv_w_rkv[j], rwkv_w0[j], rwkv_w1[j], rwkv_w2[j],
                               rwkv_a0[j], rwkv_a1[j], rwkv_a2[j], rwkv_g1[j], rwkv_g2[j],
                               rwkv_k_k[j], rwkv_k_a[j], rwkv_r_k[j], rwkv_ln_w[j], rwkv_ln_b[j],
                               rwkv_w_out[j])
        elif m == 2:
            h = mamba2_ssd(h, ssd_w_in[j], ssd_conv_w[j], ssd_conv_b[j], ssd_dt_bias[j],
                           ssd_a_log[j], ssd_d[j], ssd_norm_w[j], ssd_w_out[j])
        else:
            h = rglru_block(h, lru_w_in[j], lru_conv_w[j], lru_conv_b[j], lru_w_gates[j],
                            lru_b_gates[j], lru_lam[j], lru_w_out[j])
        x = x + rmsnorm(h, g[3])
        x = x + 0.5 * rmsnorm(swiglu(rmsnorm(x, g[4]), ffn_w_in[i, 1], ffn_w_out[i, 1]), g[5])
    return x
```

```python
import functools

import jax
import jax.numpy as jnp
from jax import lax
from jax.experimental import pallas as pl
from jax.experimental.pallas import tpu as pltpu

F32 = jnp.float32
BF16 = jnp.bfloat16

V7X_VMEM_BYTES = 64 * 1024 * 1024
SUBLANES = 8
LANES = 128

NORM_EPS = 1e-6
L2_EPS = 1e-6
SSD_NORM_EPS = 1e-5
RWKV_GN_EPS = 64e-5
LRU_C = 8.0

ROW_TILE = 512
TIME_TILE = 256
GDN_CHUNK = 64
RWKV_CHUNK = 64
SSD_CHUNK = 128
CONV_TAPS = 4


def _params(n_axes, vmem_mib):
    return pltpu.CompilerParams(
        dimension_semantics=("arbitrary",) * n_axes,
        vmem_limit_bytes=vmem_mib * 1024 * 1024,
    )


def _resident(shape):
    zeros = (0,) * len(shape)
    return pl.BlockSpec(shape, lambda *_: zeros, pipeline_mode=pl.Buffered(1))


def _rms(x, g, eps):
    return x * lax.rsqrt(jnp.mean(x * x, axis=-1, keepdims=True) + eps) * g


def _sigmoid(x):
    return 1.0 / (1.0 + jnp.exp(-x))


def _silu(x):
    return x * _sigmoid(x)


def _softplus(x):
    return jnp.maximum(x, 0.0) + jnp.log1p(jnp.exp(-jnp.abs(x)))


def _gelu_tanh(x):
    return 0.5 * x * (1.0 + jnp.tanh(0.7978845608028654 * (x + 0.044715 * (x * x * x))))


def _dot(a, b):
    return jnp.dot(a.astype(BF16), b.astype(BF16), preferred_element_type=F32)


def _dot_nt(a, b):
    return lax.dot_general(a.astype(BF16), b.astype(BF16), (((1,), (1,)), ((), ())),
                           preferred_element_type=F32)


def _dot_tn(a, b):
    return lax.dot_general(a.astype(BF16), b.astype(BF16), (((0,), (0,)), ((), ())),
                           preferred_element_type=F32)


def _seg_cumsum(x, axis, seg):
    pos = lax.broadcasted_iota(jnp.int32, x.shape, axis) & (seg - 1)
    s = 1
    while s < seg:
        x = x + jnp.where(pos >= s, pltpu.roll(x, s, axis), 0.0)
        s *= 2
    return x


def _lin_scan(a, b):
    n = a.shape[0]
    pos = lax.broadcasted_iota(jnp.int32, a.shape, 0)
    s = 1
    while s < n:
        keep = pos >= s
        b = jnp.where(keep, a * pltpu.roll(b, s, 0) + b, b)
        a = jnp.where(keep, a * pltpu.roll(a, s, 0), a)
        s *= 2
    return a, b


def _causal_conv(pre, tail_ref, w):
    tt = pre.shape[0]
    ext = jnp.concatenate([tail_ref[...], pre], axis=0)
    y = pre * w[CONV_TAPS - 1:CONV_TAPS, :]
    for k in range(1, CONV_TAPS):
        y = y + pltpu.roll(ext, k, 0)[SUBLANES:, :] * w[CONV_TAPS - 1 - k:CONV_TAPS - k, :]
    tail_ref[...] = pre[tt - SUBLANES:, :]
    return y


def _tri_inv(low):
    n = low.shape[0]
    row = lax.broadcasted_iota(jnp.int32, (n, n), 0)
    col = lax.broadcasted_iota(jnp.int32, (n, n), 1)
    eye = jnp.where(row == col, 1.0, 0.0)
    t = eye - jnp.where((row >> 1) == (col >> 1), low, 0.0)
    b, sh = 2, 1
    while b < n:
        sub = ((row >> (sh + 1)) == (col >> (sh + 1))) & (((row >> sh) & 1) == 1) & (((col >> sh) & 1) == 0)
        a21 = jnp.where(sub, low, 0.0)
        t = t - _dot(_dot(t, a21), t)
        b, sh = 2 * b, sh + 1
    return t


def _ffn_kernel(x_ref, g0_ref, g1_ref, win_ref, wout_ref, o_ref, *, d_ff, n_split):
    x = x_ref[...]
    xn = _rms(x, g0_ref[...], NORM_EPS).astype(BF16)
    fc = d_ff // n_split
    acc = None
    for c in range(n_split):
        gate = jnp.dot(xn, win_ref[:, c * fc:(c + 1) * fc], preferred_element_type=F32)
        up = jnp.dot(xn, win_ref[:, d_ff + c * fc:d_ff + (c + 1) * fc], preferred_element_type=F32)
        act = (_silu(gate) * up).astype(BF16)
        part = jnp.dot(act, wout_ref[c * fc:(c + 1) * fc, :], preferred_element_type=F32)
        acc = part if acc is None else acc + part
    o_ref[...] = x + 0.5 * _rms(acc, g1_ref[...], NORM_EPS)


def _ffn(x2d, g0, g1, w_in, w_out):
    n, d = x2d.shape
    d_ff = w_out.shape[0]
    n_split = 2 if (d_ff // 2) % LANES == 0 else 1
    row = pl.BlockSpec((ROW_TILE, d), lambda i: (i, 0))
    return pl.pallas_call(
        functools.partial(_ffn_kernel, d_ff=d_ff, n_split=n_split),
        grid=(n // ROW_TILE,),
        in_specs=[row, _resident((1, d)), _resident((1, d)), _resident(w_in.shape), _resident(w_out.shape)],
        out_specs=row,
        out_shape=jax.ShapeDtypeStruct((n, d), F32),
        compiler_params=_params(1, 56),
        name="ffn",
    )(x2d, g0.reshape(1, d), g1.reshape(1, d), w_in, w_out)


def _norm_matmul_kernel(x_ref, g_ref, *refs, n_w):
    xn = _rms(x_ref[...], g_ref[...], NORM_EPS).astype(BF16)
    for w_ref, o_ref in zip(refs[:n_w], refs[n_w:]):
        o_ref[...] = jnp.dot(xn, w_ref[...], preferred_element_type=F32).astype(o_ref.dtype)


def _norm_matmul(x2d, g, weights, name):
    n, d = x2d.shape
    return pl.pallas_call(
        functools.partial(_norm_matmul_kernel, n_w=len(weights)),
        grid=(n // ROW_TILE,),
        in_specs=[pl.BlockSpec((ROW_TILE, d), lambda i: (i, 0)), _resident((1, d))]
        + [_resident(w.shape) for w in weights],
        out_specs=[pl.BlockSpec((ROW_TILE, w.shape[1]), lambda i: (i, 0)) for w in weights],
        out_shape=[jax.ShapeDtypeStruct((n, w.shape[1]), F32) for w in weights],
        compiler_params=_params(1, 48),
        name=name,
    )(x2d, g.reshape(1, d), *weights)


def _out_proj_kernel(y_ref, w_ref, g_ref, x_ref, o_ref):
    h = jnp.dot(y_ref[...], w_ref[...], preferred_element_type=F32)
    o_ref[...] = x_ref[...] + _rms(h, g_ref[...], NORM_EPS)


def _out_proj(y2d, w, g, x2d, name):
    n, d = x2d.shape
    k = y2d.shape[1]
    return pl.pallas_call(
        _out_proj_kernel,
        grid=(n // ROW_TILE,),
        in_specs=[pl.BlockSpec((ROW_TILE, k), lambda i: (i, 0)), _resident(w.shape), _resident((1, d)),
                  pl.BlockSpec((ROW_TILE, d), lambda i: (i, 0))],
        out_specs=pl.BlockSpec((ROW_TILE, d), lambda i: (i, 0)),
        out_shape=jax.ShapeDtypeStruct((n, d), F32),
        compiler_params=_params(1, 32),
        name=name,
    )(y2d, w, g.reshape(1, d), x2d)


def _gdn_kernel(q_ref, k_ref, v_ref, z_ref, bac_ref, bar_ref, cwq_ref, cwk_ref, cwv_ref, alog_ref, dtb_ref,
                nw_ref, y_ref, tq_ref, tk_ref, tv_ref, s_ref, *, hb, dk, dv, chunk):
    @pl.when(pl.program_id(2) == 0)
    def _init():
        tq_ref[...] = jnp.zeros_like(tq_ref)
        tk_ref[...] = jnp.zeros_like(tk_ref)
        tv_ref[...] = jnp.zeros_like(tv_ref)
        s_ref[...] = jnp.zeros_like(s_ref)

    q = _silu(_causal_conv(q_ref[0], tq_ref, cwq_ref[...]))
    k = _silu(_causal_conv(k_ref[0], tk_ref, cwk_ref[...]))
    v = _silu(_causal_conv(v_ref[0], tv_ref, cwv_ref[...]))
    z = z_ref[0]
    tt = q.shape[0]
    row = lax.broadcasted_iota(jnp.int32, (chunk, chunk), 0)
    col = lax.broadcasted_iota(jnp.int32, (chunk, chunk), 1)
    causal = row >= col
    strict = row > col
    outs = []
    for h in range(hb):
        qh = q[:, h * dk:(h + 1) * dk]
        kh = k[:, h * dk:(h + 1) * dk]
        vh = v[:, h * dv:(h + 1) * dv]
        qh = qh * (lax.rsqrt(jnp.sum(qh * qh, axis=-1, keepdims=True) + L2_EPS) * dk ** -0.5)
        kh = kh * lax.rsqrt(jnp.sum(kh * kh, axis=-1, keepdims=True) + L2_EPS)
        ba_c = bac_ref[0, h]
        ba_r = bar_ref[0, h]
        beta = _sigmoid(ba_c[:, 0:1])
        neg_a = -jnp.exp(alog_ref[h])
        g_col = neg_a * _softplus(ba_c[:, 1:2] + dtb_ref[h])
        g_row = neg_a * _softplus(ba_r[1:2, :] + dtb_ref[h])
        gc_col = _seg_cumsum(g_col, 0, chunk)
        gc_row = _seg_cumsum(g_row, 1, chunk)
        state = s_ref[h]
        o_chunks = []
        for c in range(tt // chunk):
            sl = slice(c * chunk, (c + 1) * chunk)
            qc, kc, vc, bc = qh[sl], kh[sl], vh[sl], beta[sl]
            gcc = gc_col[sl]
            gcr = gc_row[:, sl]
            decay = jnp.exp(jnp.where(causal, gcc - gcr, -jnp.inf))
            lower = jnp.where(strict, _dot_nt(kc, kc) * decay * bc, 0.0)
            t_inv = _tri_inv(lower)
            eg = jnp.exp(gcc)
            sol = _dot(t_inv, jnp.concatenate([vc * bc, kc * (bc * eg)], axis=1))
            u, w = sol[:, :dv], sol[:, dv:]
            qk = _dot_nt(qc, kc) * decay
            g_last = gcc[chunk - 1:chunk, :]
            v_new = u - _dot(w, state)
            o_chunks.append(_dot(qc * eg, state) + _dot(qk, v_new))
            state = state * jnp.exp(g_last) + _dot_tn(kc * jnp.exp(g_last - gcc), v_new)
        s_ref[h] = state
        o = jnp.concatenate(o_chunks, axis=0)
        outs.append(_rms(o, nw_ref[...], NORM_EPS) * _silu(z[:, h * dv:(h + 1) * dv]))
    y_ref[0] = jnp.concatenate(outs, axis=1).astype(y_ref.dtype)


def _gdn_core(qkv, z, ba, conv_w, a_log, dt_bias, norm_w, *, heads, dk, dv):
    bsz, t, _ = z.shape
    hb = 2
    nhb = heads // hb
    hw = hb * dk
    ba = ba.reshape(bsz, t, 2, heads)
    ba_col = ba.transpose(0, 3, 1, 2)
    ba_row = ba.transpose(0, 3, 2, 1)

    def col(off):
        return pl.BlockSpec((1, TIME_TILE, hw), lambda b, j, i: (b, i, off + j))

    def wcol(off):
        return pl.BlockSpec((CONV_TAPS, hw), lambda b, j, i: (0, off + j))

    per_head = pl.BlockSpec((hb, 1, 1), lambda b, j, i: (j, 0, 0))
    return pl.pallas_call(
        functools.partial(_gdn_kernel, hb=hb, dk=dk, dv=dv, chunk=GDN_CHUNK),
        grid=(bsz, nhb, t // TIME_TILE),
        in_specs=[col(0), col(nhb), col(2 * nhb), col(0),
                  pl.BlockSpec((1, hb, TIME_TILE, 2), lambda b, j, i: (b, j, i, 0)),
                  pl.BlockSpec((1, hb, 2, TIME_TILE), lambda b, j, i: (b, j, 0, i)),
                  wcol(0), wcol(nhb), wcol(2 * nhb), per_head, per_head,
                  pl.BlockSpec((1, dv), lambda b, j, i: (0, 0))],
        out_specs=col(0),
        out_shape=jax.ShapeDtypeStruct((bsz, t, heads * dv), BF16),
        scratch_shapes=[pltpu.VMEM((SUBLANES, hw), F32)] * 3 + [pltpu.VMEM((hb, dk, dv), F32)],
        compiler_params=_params(3, 32),
        name="gdn_core",
    )(qkv, qkv, qkv, z, ba_col, ba_row, conv_w, conv_w, conv_w,
      a_log.reshape(heads, 1, 1), dt_bias.reshape(heads, 1, 1), norm_w.reshape(1, dv))


def _gdn_mixer(x2d, g_in, g_out, shape, w_in, conv_w, a_log, dt_bias, norm_w, w_out):
    bsz, t, d = shape
    heads = a_log.shape[0]
    dv = norm_w.shape[0]
    n_qkv = conv_w.shape[1]
    dk = (n_qkv - heads * dv) // (2 * heads)
    w = w_in.astype(BF16)
    qkv, z, ba = _norm_matmul(
        x2d, g_in, [w[:, :n_qkv], w[:, n_qkv:n_qkv + heads * dv], w[:, n_qkv + heads * dv:]], "gdn_in")
    y = _gdn_core(qkv.reshape(bsz, t, -1), z.reshape(bsz, t, -1), ba.reshape(bsz, t, -1), conv_w, a_log, dt_bias,
                  norm_w, heads=heads, dk=dk, dv=dv)
    return _out_proj(y.reshape(bsz * t, -1), w_out.astype(BF16), g_out, x2d, "gdn_out")


def _rwkv_in_kernel(x_ref, g_ref, mu_ref, wr_ref, wk_ref, wv_ref, w0_ref, w1_ref, w2_ref, a0_ref, a1_ref, a2_ref,
                    g1_ref, g2_ref, r_ref, k_ref, v_ref, w_ref, a_ref, go_ref, prev_ref):
    @pl.when(pl.program_id(1) == 0)
    def _init():
        prev_ref[...] = jnp.zeros_like(prev_ref)

    hn = _rms(x_ref[0], g_ref[...], NORM_EPS)
    tm = hn.shape[0]
    ext = jnp.concatenate([prev_ref[...], hn], axis=0)
    xx = pltpu.roll(ext, 1, 0)[SUBLANES:, :] - hn
    prev_ref[...] = hn[tm - SUBLANES:, :]
    mu = mu_ref[...]

    def mix(i):
        return (hn + xx * mu[i:i + 1, :]).astype(BF16)

    r_ref[0] = jnp.dot(mix(0), wr_ref[...], preferred_element_type=F32)
    k_ref[0] = jnp.dot(mix(1), wk_ref[...], preferred_element_type=F32)
    v_ref[0] = jnp.dot(mix(2), wv_ref[...], preferred_element_type=F32)
    w_ref[0] = w0_ref[...] + _dot(jnp.tanh(jnp.dot(mix(3), w1_ref[...], preferred_element_type=F32)), w2_ref[...])
    a_ref[0] = a0_ref[...] + _dot(jnp.dot(mix(4), a1_ref[...], preferred_element_type=F32), a2_ref[...])
    go_ref[0] = _dot(_sigmoid(jnp.dot(mix(5), g1_ref[...], preferred_element_type=F32)), g2_ref[...])


def _rwkv_core_kernel(r_ref, k_ref, v_ref, w_ref, a_ref, g_ref, kk_ref, ka_ref, rk_ref, lnw_ref, lnb_ref,
                      y_ref, s_ref, *, hd, chunk):
    @pl.when(pl.program_id(2) == 0)
    def _init():
        s_ref[...] = jnp.zeros_like(s_ref)

    r, k, v = r_ref[0], k_ref[0], v_ref[0]
    tt, width = r.shape
    n_heads = width // hd
    lane = lax.broadcasted_iota(jnp.int32, (1, width), 1)
    head_of_lane = [(lane >= j * hd) & (lane < (j + 1) * hd) for j in range(n_heads)]

    def head_sum(x):
        tot = jnp.zeros_like(x)
        for m in head_of_lane:
            tot = tot + jnp.where(m, jnp.sum(jnp.where(m, x, 0.0), axis=-1, keepdims=True), 0.0)
        return tot

    def by_head(parts):
        out = parts[-1]
        for m, p in zip(head_of_lane[:-1], parts[:-1]):
            out = jnp.where(m, p, out)
        return out

    log_w = -jnp.exp(-_softplus(-w_ref[0]) - 0.5)
    a = _sigmoid(a_ref[0])
    kk = k * kk_ref[...]
    kk = kk * lax.rsqrt(head_sum(kk * kk) + L2_EPS)
    k2 = k * (1.0 + (a - 1.0) * ka_ref[...])
    cum = _seg_cumsum(log_w, 0, chunk)
    inv = jnp.exp(-cum)
    a_t = -kk * jnp.exp(cum - log_w)
    b_t = kk * a * inv
    k_t = k2 * inv
    r_t = r * jnp.exp(cum)

    rowi = lax.broadcasted_iota(jnp.int32, (chunk, chunk), 0)
    coli = lax.broadcasted_iota(jnp.int32, (chunk, chunk), 1)
    causal = rowi >= coli
    strict = rowi > coli
    srow = lax.broadcasted_iota(jnp.int32, (width, width), 0)
    scol = lax.broadcasted_iota(jnp.int32, (width, width), 1)
    same_head = None
    for j in range(n_heads):
        blk = (srow >= j * hd) & (srow < (j + 1) * hd) & (scol >= j * hd) & (scol < (j + 1) * hd)
        same_head = blk if same_head is None else (same_head | blk)

    state = s_ref[...]
    ys = []
    for c in range(tt // chunk):
        sl = slice(c * chunk, (c + 1) * chunk)
        ac, bc, kc, rc, vc = a_t[sl], b_t[sl], k_t[sl], r_t[sl], v[sl]
        a_s = _dot_nt(ac, state)
        r_s = _dot_nt(rc, state)
        u_parts, p_rb, p_rk = [], [], []
        for m in head_of_lane:
            aj = jnp.where(m, ac, 0.0)
            rj = jnp.where(m, rc, 0.0)
            a_ab = jnp.where(strict, _dot_nt(aj, bc), 0.0)
            a_ak = jnp.where(strict, _dot_nt(aj, kc), 0.0)
            p_rb.append(jnp.where(causal, _dot_nt(rj, bc), 0.0))
            p_rk.append(jnp.where(causal, _dot_nt(rj, kc), 0.0))
            u_parts.append(_dot(_tri_inv(-a_ab), a_s + _dot(a_ak, vc)))
        u = by_head(u_parts)
        ys.append(r_s + by_head([_dot(pb, u) + _dot(pk, vc) for pb, pk in zip(p_rb, p_rk)]))
        decay_end = jnp.exp(cum[(c + 1) * chunk - 1:(c + 1) * chunk, :])
        state = jnp.where(same_head, state + _dot_tn(u, bc) + _dot_tn(vc, kc), 0.0) * decay_end
    s_ref[...] = state

    y = jnp.concatenate(ys, axis=0)
    mean = head_sum(y) * (1.0 / hd)
    yc = y - mean
    var = head_sum(yc * yc) * (1.0 / hd)
    y = yc * lax.rsqrt(var + RWKV_GN_EPS) * lnw_ref[...] + lnb_ref[...]
    bonus = head_sum(r * k2 * rk_ref[...]) * v
    y_ref[0] = ((y + bonus) * g_ref[0]).astype(y_ref.dtype)


def _rwkv_mixer(x2d, g_in, g_out, shape, mu, w_rkv, w0, w1, w2, a0, a1, a2, g1, g2, k_k, k_a, r_k, ln_w, ln_b,
                w_out):
    bsz, t, d = shape
    hd = r_k.shape[1]
    x3 = x2d.reshape(bsz, t, d)
    wb = w_rkv.astype(BF16)
    tile = pl.BlockSpec((1, ROW_TILE, d), lambda b, i: (b, i, 0))
    vec = _resident((1, d))
    r, k, v, w, a, g = pl.pallas_call(
        _rwkv_in_kernel,
        grid=(bsz, t // ROW_TILE),
        in_specs=[tile, vec, _resident(mu.shape), _resident((d, d)), _resident((d, d)), _resident((d, d)),
                  vec, _resident(w1.shape), _resident(w2.shape), vec, _resident(a1.shape), _resident(a2.shape),
                  _resident(g1.shape), _resident(g2.shape)],
        out_specs=[tile] * 6,
        out_shape=[jax.ShapeDtypeStruct((bsz, t, d), F32)] * 6,
        scratch_shapes=[pltpu.VMEM((SUBLANES, d), F32)],
        compiler_params=_params(2, 48),
        name="rwkv_in",
    )(x3, g_in.reshape(1, d), mu, wb[0], wb[1], wb[2], w0.reshape(1, d), w1.astype(BF16), w2.astype(BF16),
      a0.reshape(1, d), a1.astype(BF16), a2.astype(BF16), g1.astype(BF16), g2.astype(BF16))

    width = LANES
    blk = pl.BlockSpec((1, TIME_TILE, width), lambda b, j, i: (b, i, j))
    pvec = pl.BlockSpec((1, width), lambda b, j, i: (0, j))
    y = pl.pallas_call(
        functools.partial(_rwkv_core_kernel, hd=hd, chunk=RWKV_CHUNK),
        grid=(bsz, d // width, t // TIME_TILE),
        in_specs=[blk] * 6 + [pvec] * 5,
        out_specs=blk,
        out_shape=jax.ShapeDtypeStruct((bsz, t, d), BF16),
        scratch_shapes=[pltpu.VMEM((width, width), F32)],
        compiler_params=_params(3, 32),
        name="rwkv_core",
    )(r, k, v, w, a, g, k_k.reshape(1, d), k_a.reshape(1, d), r_k.reshape(1, d), ln_w.reshape(1, d),
      ln_b.reshape(1, d))
    return _out_proj(y.reshape(bsz * t, d), w_out.astype(BF16), g_out, x2d, "rwkv_out")


def _ssd_kernel(z_ref, x_ref, b_ref, c_ref, dtc_ref, dtr_ref, cwx_ref, cbx_ref, cwb_ref, cbb_ref, cwc_ref, cbc_ref,
                dtbc_ref, dtbr_ref, alc_ref, alr_ref, dskip_ref, nw_ref, y_ref, tx_ref, tb_ref, tc_ref, h_ref,
                *, heads, hp, chunk):
    @pl.when(pl.program_id(2) == 0)
    def _init():
        tx_ref[...] = jnp.zeros_like(tx_ref)
        tb_ref[...] = jnp.zeros_like(tb_ref)
        tc_ref[...] = jnp.zeros_like(tc_ref)
        h_ref[...] = jnp.zeros_like(h_ref)

    xs = _silu(_causal_conv(x_ref[0], tx_ref, cwx_ref[...]) + cbx_ref[...])
    bm = _silu(_causal_conv(b_ref[0], tb_ref, cwb_ref[...]) + cbb_ref[...])
    cm = _silu(_causal_conv(c_ref[0], tc_ref, cwc_ref[...]) + cbc_ref[...])
    tt, width = xs.shape
    dt_c = _softplus(dtc_ref[0, 0] + dtbc_ref[0])
    dt_r = _softplus(dtr_ref[0, 0] + dtbr_ref[0])
    da_c = dt_c * -jnp.exp(alc_ref[0])
    da_r = dt_r * -jnp.exp(alr_ref[0])
    acs_c = _seg_cumsum(da_c, 0, chunk)
    acs_r = _seg_cumsum(da_r, 1, chunk)

    lane = lax.broadcasted_iota(jnp.int32, (1, width), 1)
    head_of_lane = [(lane >= e * hp) & (lane < (e + 1) * hp) for e in range(heads)]

    def expand(cols):
        out = jnp.broadcast_to(cols[:, heads - 1:heads], (cols.shape[0], width))
        for e in range(heads - 1):
            out = jnp.where(head_of_lane[e], cols[:, e:e + 1], out)
        return out

    rowi = lax.broadcasted_iota(jnp.int32, (chunk, chunk), 0)
    coli = lax.broadcasted_iota(jnp.int32, (chunk, chunk), 1)
    causal = rowi >= coli
    pair_lo = lax.broadcasted_iota(jnp.int32, (1, 2 * hp), 1) < hp

    state = h_ref[...]
    ys = []
    for c in range(tt // chunk):
        sl = slice(c * chunk, (c + 1) * chunk)
        xc, bc, cc = xs[sl], bm[sl], cm[sl]
        ac_c = acs_c[sl]
        ac_r = acs_r[:, sl]
        tot = ac_c[chunk - 1:chunk, :]
        xdt = xc * expand(dt_c[sl])
        cb = _dot_nt(cc, bc)
        y_pairs = []
        for p in range(heads // 2):
            xp = xdt[:, 2 * p * hp:2 * (p + 1) * hp]
            halves = []
            for e in (2 * p, 2 * p + 1):
                seg = jnp.where(causal, ac_c[:, e:e + 1] - ac_r[e:e + 1, :], -jnp.inf)
                halves.append(_dot(cb * jnp.exp(seg), xp))
            y_pairs.append(jnp.where(pair_lo, halves[0], halves[1]))
        y_diag = jnp.concatenate(y_pairs, axis=1)
        y_off = _dot(cc, state) * expand(jnp.exp(ac_c))
        new = _dot_tn(bc, xdt * expand(jnp.exp(tot - ac_c)))
        state = state * expand(jnp.exp(tot)) + new
        ys.append(y_diag + y_off + xc * dskip_ref[0])
    h_ref[...] = state

    y = jnp.concatenate(ys, axis=0) * _silu(z_ref[0])
    y_ref[0] = _rms(y, nw_ref[0], SSD_NORM_EPS).astype(y_ref.dtype)


def _ssd_mixer(x2d, g_in, g_out, shape, w_in, conv_w, conv_b, dt_bias, a_log, d_skip, norm_w, w_out):
    bsz, t, d = shape
    n_heads = a_log.shape[0]
    inner = norm_w.shape[0]
    hp = inner // n_heads
    n_xbc = conv_w.shape[1]
    groups = 4
    ns = (n_xbc - inner) // (2 * groups)
    gh = n_heads // groups
    gw = inner // groups
    w = w_in.astype(BF16)
    o_b, o_c, o_dt = 2 * inner, 2 * inner + groups * ns, inner + n_xbc
    z, xs, bm, cm, dt = _norm_matmul(
        x2d, g_in, [w[:, :inner], w[:, inner:o_b], w[:, o_b:o_c], w[:, o_c:o_dt], w[:, o_dt:]], "ssd_in")
    dt = dt.reshape(bsz, t, groups, gh)
    dt_col = dt.transpose(0, 2, 1, 3)
    dt_row = dt.transpose(0, 2, 3, 1)

    def col(width):
        return pl.BlockSpec((1, TIME_TILE, width), lambda b, j, i: (b, i, j))

    def wspec(rows, width, off):
        return pl.BlockSpec((rows, width), lambda b, j, i: (0, off + j))

    def grp(shape2):
        return pl.BlockSpec((1,) + shape2, lambda b, j, i: (j, 0, 0))

    cw_x, cw_b, cw_c = conv_w[:, :inner], conv_w[:, inner:inner + groups * ns], conv_w[:, inner + groups * ns:]
    cb = conv_b.reshape(1, n_xbc)
    cb_x, cb_b, cb_c = cb[:, :inner], cb[:, inner:inner + groups * ns], cb[:, inner + groups * ns:]
    y = pl.pallas_call(
        functools.partial(_ssd_kernel, heads=gh, hp=hp, chunk=SSD_CHUNK),
        grid=(bsz, groups, t // TIME_TILE),
        in_specs=[col(gw), col(gw), col(ns), col(ns),
                  pl.BlockSpec((1, 1, TIME_TILE, gh), lambda b, j, i: (b, j, i, 0)),
                  pl.BlockSpec((1, 1, gh, TIME_TILE), lambda b, j, i: (b, j, 0, i)),
                  wspec(CONV_TAPS, gw, 0), wspec(1, gw, 0), wspec(CONV_TAPS, ns, 0), wspec(1, ns, 0),
                  wspec(CONV_TAPS, ns, 0), wspec(1, ns, 0),
                  grp((1, gh)), grp((gh, 1)), grp((1, gh)), grp((gh, 1)), grp((1, gw)), grp((1, gw))],
        out_specs=col(gw),
        out_shape=jax.ShapeDtypeStruct((bsz, t, inner), BF16),
        scratch_shapes=[pltpu.VMEM((SUBLANES, gw), F32), pltpu.VMEM((SUBLANES, ns), F32),
                        pltpu.VMEM((SUBLANES, ns), F32), pltpu.VMEM((ns, gw), F32)],
        compiler_params=_params(3, 32),
        name="ssd_core",
    )(z.reshape(bsz, t, inner), xs.reshape(bsz, t, inner), bm.reshape(bsz, t, -1), cm.reshape(bsz, t, -1),
      dt_col, dt_row, cw_x, cb_x, cw_b, cb_b, cw_c, cb_c,
      dt_bias.reshape(groups, 1, gh), dt_bias.reshape(groups, gh, 1),
      a_log.reshape(groups, 1, gh), a_log.reshape(groups, gh, 1),
      jnp.repeat(d_skip, hp).reshape(groups, 1, gw), norm_w.reshape(groups, 1, gw))
    return _out_proj(y.reshape(bsz * t, inner), w_out.astype(BF16), g_out, x2d, "ssd_out")


def _lru_kernel(gate_ref, xb_ref, cw_ref, cb_ref, wg_ref, bg_ref, lam_ref, y_ref, tail_ref, h_ref):
    @pl.when(pl.program_id(2) == 0)
    def _init():
        tail_ref[...] = jnp.zeros_like(tail_ref)
        h_ref[...] = jnp.zeros_like(h_ref)

    u = _causal_conv(xb_ref[0], tail_ref, cw_ref[...]) + cb_ref[...]
    tt, bw = u.shape
    gates = _dot(u, wg_ref[0])
    r = _sigmoid(gates[:, :bw] + bg_ref[0:1, :])
    i = _sigmoid(gates[:, bw:] + bg_ref[1:2, :])
    log_a = -LRU_C * r * _softplus(-lam_ref[...])
    a = jnp.exp(log_a)
    bterm = jnp.sqrt(1.0 - jnp.exp(2.0 * log_a)) * (i * u)
    a_cum, h_loc = _lin_scan(a, bterm)
    h = h_loc + a_cum * h_ref[...]
    h_ref[...] = h[tt - 1:tt, :]
    y_ref[0] = (_gelu_tanh(gate_ref[0]) * h).astype(y_ref.dtype)


def _lru_mixer(x2d, g_in, g_out, shape, w_in, conv_w, conv_b, w_gates, b_gates, lam, w_out):
    bsz, t, d = shape
    width = lam.shape[0]
    blocks, bw, _ = w_gates.shape
    w = w_in.astype(BF16)
    gate, xb = _norm_matmul(x2d, g_in, [w[:, :width], w[:, width:]], "lru_in")
    blk = pl.BlockSpec((1, TIME_TILE, bw), lambda b, j, i: (b, i, j))

    def pvec(rows):
        return pl.BlockSpec((rows, bw), lambda b, j, i: (0, j))

    y = pl.pallas_call(
        _lru_kernel,
        grid=(bsz, blocks, t // TIME_TILE),
        in_specs=[blk, blk, pvec(CONV_TAPS), pvec(1),
                  pl.BlockSpec((1, bw, 2 * bw), lambda b, j, i: (j, 0, 0)), pvec(2), pvec(1)],
        out_specs=blk,
        out_shape=jax.ShapeDtypeStruct((bsz, t, width), BF16),
        scratch_shapes=[pltpu.VMEM((SUBLANES, bw), F32), pltpu.VMEM((1, bw), F32)],
        compiler_params=_params(3, 32),
        name="lru_core",
    )(gate.reshape(bsz, t, width), xb.reshape(bsz, t, width), conv_w, conv_b.reshape(1, width),
      w_gates.astype(BF16), b_gates, lam.reshape(1, width))
    return _out_proj(y.reshape(bsz * t, width), w_out.astype(BF16), g_out, x2d, "lru_out")


def kernel(x, norm_g, ffn_w_in, ffn_w_out, gdn_w_in, gdn_conv_w, gdn_a_log, gdn_dt_bias, gdn_norm_w, gdn_w_out, rwkv_mu, rwkv_w_rkv, rwkv_w0, rwkv_w1, rwkv_w2, rwkv_a0, rwkv_a1, rwkv_a2, rwkv_g1, rwkv_g2, rwkv_k_k, rwkv_k_a, rwkv_r_k, rwkv_ln_w, rwkv_ln_b, rwkv_w_out, ssd_w_in, ssd_conv_w, ssd_conv_b, ssd_dt_bias, ssd_a_log, ssd_d, ssd_norm_w, ssd_w_out, lru_w_in, lru_conv_w, lru_conv_b, lru_w_gates, lru_b_gates, lru_lam, lru_w_out):
    shape = x.shape
    bsz, t, d = shape
    depth = norm_g.shape[0]
    n_mixers = 4
    x2d = x.reshape(bsz * t, d)
    for i in range(depth):
        m, j = i % n_mixers, i // n_mixers
        g = norm_g[i]
        x2d = _ffn(x2d, g[0], g[1], ffn_w_in[i, 0].astype(BF16), ffn_w_out[i, 0].astype(BF16))
        if m == 0:
            x2d = _gdn_mixer(x2d, g[2], g[3], shape, gdn_w_in[j], gdn_conv_w[j], gdn_a_log[j], gdn_dt_bias[j],
                             gdn_norm_w[j], gdn_w_out[j])
        elif m == 1:
            x2d = _rwkv_mixer(x2d, g[2], g[3], shape, rwkv_mu[j], rwkv_w_rkv[j], rwkv_w0[j], rwkv_w1[j],
                              rwkv_w2[j], rwkv_a0[j], rwkv_a1[j], rwkv_a2[j], rwkv_g1[j], rwkv_g2[j], rwkv_k_k[j],
                              rwkv_k_a[j], rwkv_r_k[j], rwkv_ln_w[j], rwkv_ln_b[j], rwkv_w_out[j])
        elif m == 2:
            x2d = _ssd_mixer(x2d, g[2], g[3], shape, ssd_w_in[j], ssd_conv_w[j], ssd_conv_b[j], ssd_dt_bias[j],
                             ssd_a_log[j], ssd_d[j], ssd_norm_w[j], ssd_w_out[j])
        else:
            x2d = _lru_mixer(x2d, g[2], g[3], shape, lru_w_in[j], lru_conv_w[j], lru_conv_b[j], lru_w_gates[j],
                             lru_b_gates[j], lru_lam[j], lru_w_out[j])
        x2d = _ffn(x2d, g[4], g[5], ffn_w_in[i, 1].astype(BF16), ffn_w_out[i, 1].astype(BF16))
    return x2d.reshape(shape)
```

```python
import functools

import jax
import jax.numpy as jnp
from jax import lax
from jax.experimental import pallas as pl
from jax.experimental.pallas import tpu as pltpu

F32 = jnp.float32
BF16 = jnp.bfloat16

V7X_VMEM_BYTES = 64 * 1024 * 1024
SUBLANES = 8
LANES = 128

NORM_EPS = 1e-6
L2_EPS = 1e-6
SSD_NORM_EPS = 1e-5
RWKV_GN_EPS = 64e-5
LRU_C = 8.0

ROW_TILE = 512
TIME_TILE = 256
GDN_CHUNK = 64
RWKV_CHUNK = 64
RWKV_WIDTH = 512
GDN_HEADS_PER_STEP = 4
SSD_CHUNK = 128
CONV_TAPS = 4


def _params(n_axes, vmem_mib):
    return pltpu.CompilerParams(
        dimension_semantics=("arbitrary",) * n_axes,
        vmem_limit_bytes=vmem_mib * 1024 * 1024,
    )


def _resident(shape):
    zeros = (0,) * len(shape)
    return pl.BlockSpec(shape, lambda *_: zeros, pipeline_mode=pl.Buffered(1))


def _rms(x, g, eps):
    return x * lax.rsqrt(jnp.mean(x * x, axis=-1, keepdims=True) + eps) * g


def _sigmoid(x):
    return 1.0 / (1.0 + jnp.exp(-x))


def _silu(x):
    return x * _sigmoid(x)


def _softplus(x):
    return jnp.maximum(x, 0.0) + jnp.log1p(jnp.exp(-jnp.abs(x)))


def _gelu_tanh(x):
    return 0.5 * x * (1.0 + jnp.tanh(0.7978845608028654 * (x + 0.044715 * (x * x * x))))


def _dot(a, b):
    return jnp.dot(a.astype(BF16), b.astype(BF16), preferred_element_type=F32)


def _dot_nt(a, b):
    return lax.dot_general(a.astype(BF16), b.astype(BF16), (((1,), (1,)), ((), ())),
                           preferred_element_type=F32)


def _dot_tn(a, b):
    return lax.dot_general(a.astype(BF16), b.astype(BF16), (((0,), (0,)), ((), ())),
                           preferred_element_type=F32)


def _seg_cumsum(x, axis, seg):
    pos = lax.broadcasted_iota(jnp.int32, x.shape, axis) & (seg - 1)
    s = 1
    while s < seg:
        x = x + jnp.where(pos >= s, pltpu.roll(x, s, axis), 0.0)
        s *= 2
    return x


def _lin_scan(a, b):
    n = a.shape[0]
    pos = lax.broadcasted_iota(jnp.int32, a.shape, 0)
    s = 1
    while s < n:
        keep = pos >= s
        b = jnp.where(keep, a * pltpu.roll(b, s, 0) + b, b)
        a = jnp.where(keep, a * pltpu.roll(a, s, 0), a)
        s *= 2
    return a, b


def _causal_conv(pre, tail_ref, w):
    tt = pre.shape[0]
    ext = jnp.concatenate([tail_ref[...], pre], axis=0)
    y = pre * w[CONV_TAPS - 1:CONV_TAPS, :]
    for k in range(1, CONV_TAPS):
        y = y + pltpu.roll(ext, k, 0)[SUBLANES:, :] * w[CONV_TAPS - 1 - k:CONV_TAPS - k, :]
    tail_ref[...] = pre[tt - SUBLANES:, :]
    return y


def _tri_inv(lows, block):
    n = lows[0].shape[0]
    n_sys = n // block
    row = lax.broadcasted_iota(jnp.int32, (n, n), 0)
    col = lax.broadcasted_iota(jnp.int32, (n, n), 1)
    on_diag = (row >> (block.bit_length() - 1)) == (col >> (block.bit_length() - 1))

    def to_diag(m_l):
        return jnp.where(on_diag, jnp.concatenate([m_l] * n_sys, axis=0), 0.0)

    row_l = lax.broadcasted_iota(jnp.int32, (block, n), 0)
    col_l = lax.broadcasted_iota(jnp.int32, (block, n), 1) & (block - 1)
    eye_l = jnp.where(row_l == col_l, 1.0, 0.0)
    pair_l = (row_l >> 1) == (col_l >> 1)
    t_ls = []
    for low in lows:
        low_l = low[:block]
        for s in range(1, n_sys):
            low_l = low_l + low[s * block:(s + 1) * block]
        t_ls.append(eye_l - jnp.where(pair_l, low_l, 0.0))
    b, sh = 2, 1
    while b < block:
        sub = ((row >> (sh + 1)) == (col >> (sh + 1))) & (((row >> sh) & 1) == 1) & (((col >> sh) & 1) == 0)
        xs = [_dot(t_l, jnp.where(sub, low, 0.0)) for t_l, low in zip(t_ls, lows)]
        t_ls = [t_l - _dot(x, to_diag(t_l)) for t_l, x in zip(t_ls, xs)]
        b, sh = 2 * b, sh + 1
    return [to_diag(t_l) for t_l in t_ls]


def _ffn_kernel(x_ref, g0_ref, g1_ref, win_ref, wout_ref, o_ref, *, d_ff, n_split):
    x = x_ref[...]
    xn = _rms(x, g0_ref[...], NORM_EPS).astype(BF16)
    fc = d_ff // n_split
    acc = None
    for c in range(n_split):
        gate = jnp.dot(xn, win_ref[:, c * fc:(c + 1) * fc], preferred_element_type=F32)
        up = jnp.dot(xn, win_ref[:, d_ff + c * fc:d_ff + (c + 1) * fc], preferred_element_type=F32)
        act = (_silu(gate) * up).astype(BF16)
        part = jnp.dot(act, wout_ref[c * fc:(c + 1) * fc, :], preferred_element_type=F32)
        acc = part if acc is None else acc + part
    o_ref[...] = x + 0.5 * _rms(acc, g1_ref[...], NORM_EPS)


def _ffn(x2d, g0, g1, w_in, w_out):
    n, d = x2d.shape
    d_ff = w_out.shape[0]
    n_split = 2 if (d_ff // 2) % LANES == 0 else 1
    row = pl.BlockSpec((ROW_TILE, d), lambda i: (i, 0))
    return pl.pallas_call(
        functools.partial(_ffn_kernel, d_ff=d_ff, n_split=n_split),
        grid=(n // ROW_TILE,),
        in_specs=[row, _resident((1, d)), _resident((1, d)), _resident(w_in.shape), _resident(w_out.shape)],
        out_specs=row,
        out_shape=jax.ShapeDtypeStruct((n, d), F32),
        compiler_params=_params(1, 56),
        name="ffn",
    )(x2d, g0.reshape(1, d), g1.reshape(1, d), w_in, w_out)


def _norm_matmul_kernel(x_ref, g_ref, *refs, n_w):
    xn = _rms(x_ref[...], g_ref[...], NORM_EPS).astype(BF16)
    for w_ref, o_ref in zip(refs[:n_w], refs[n_w:]):
        o_ref[...] = jnp.dot(xn, w_ref[...], preferred_element_type=F32).astype(o_ref.dtype)


def _norm_matmul(x2d, g, weights, name):
    n, d = x2d.shape
    return pl.pallas_call(
        functools.partial(_norm_matmul_kernel, n_w=len(weights)),
        grid=(n // ROW_TILE,),
        in_specs=[pl.BlockSpec((ROW_TILE, d), lambda i: (i, 0)), _resident((1, d))]
        + [_resident(w.shape) for w in weights],
        out_specs=[pl.BlockSpec((ROW_TILE, w.shape[1]), lambda i: (i, 0)) for w in weights],
        out_shape=[jax.ShapeDtypeStruct((n, w.shape[1]), F32) for w in weights],
        compiler_params=_params(1, 48),
        name=name,
    )(x2d, g.reshape(1, d), *weights)


def _out_proj_kernel(y_ref, w_ref, g_ref, x_ref, o_ref):
    h = jnp.dot(y_ref[...], w_ref[...], preferred_element_type=F32)
    o_ref[...] = x_ref[...] + _rms(h, g_ref[...], NORM_EPS)


def _out_proj(y2d, w, g, x2d, name):
    n, d = x2d.shape
    k = y2d.shape[1]
    return pl.pallas_call(
        _out_proj_kernel,
        grid=(n // ROW_TILE,),
        in_specs=[pl.BlockSpec((ROW_TILE, k), lambda i: (i, 0)), _resident(w.shape), _resident((1, d)),
                  pl.BlockSpec((ROW_TILE, d), lambda i: (i, 0))],
        out_specs=pl.BlockSpec((ROW_TILE, d), lambda i: (i, 0)),
        out_shape=jax.ShapeDtypeStruct((n, d), F32),
        compiler_params=_params(1, 32),
        name=name,
    )(y2d, w, g.reshape(1, d), x2d)


def _gdn_chunk_local(q, k, v, beta, gc_col, gc_row, causal, strict):
    decay = jnp.exp(jnp.where(causal, gc_col - gc_row, -jnp.inf))
    lower = jnp.where(strict, _dot_nt(k, k) * decay * beta, 0.0)
    eg = jnp.exp(gc_col)
    rhs = jnp.concatenate([v * beta, k * (beta * eg)], axis=1)
    return lower, rhs, _dot_nt(q, k) * decay, q * eg


def _gdn_kernel(q_ref, k_ref, v_ref, z_ref, bac_ref, bar_ref, cwq_ref, cwk_ref, cwv_ref, alog_ref, dtb_ref,
                nw_ref, y_ref, tq_ref, tk_ref, tv_ref, s_ref, *, hb, dk, dv, chunk):
    @pl.when(pl.program_id(2) == 0)
    def _init():
        tq_ref[...] = jnp.zeros_like(tq_ref)
        tk_ref[...] = jnp.zeros_like(tk_ref)
        tv_ref[...] = jnp.zeros_like(tv_ref)
        s_ref[...] = jnp.zeros_like(s_ref)

    q = _silu(_causal_conv(q_ref[0], tq_ref, cwq_ref[...]))
    k = _silu(_causal_conv(k_ref[0], tk_ref, cwk_ref[...]))
    v = _silu(_causal_conv(v_ref[0], tv_ref, cwv_ref[...]))
    tt = q.shape[0]
    shift = chunk.bit_length() - 1
    row = lax.broadcasted_iota(jnp.int32, (tt, tt), 0)
    col = lax.broadcasted_iota(jnp.int32, (tt, tt), 1)
    same_chunk = (row >> shift) == (col >> shift)
    causal = same_chunk & (row >= col)
    strict = same_chunk & (row > col)
    heads = []
    for h in range(hb):
        qh = q[:, h * dk:(h + 1) * dk]
        kh = k[:, h * dk:(h + 1) * dk]
        vh = v[:, h * dv:(h + 1) * dv]
        qh = qh * (lax.rsqrt(jnp.sum(qh * qh, axis=-1, keepdims=True) + L2_EPS) * dk ** -0.5)
        kh = kh * lax.rsqrt(jnp.sum(kh * kh, axis=-1, keepdims=True) + L2_EPS)
        ba_c = bac_ref[0, h]
        ba_r = bar_ref[0, h]
        beta = _sigmoid(ba_c[:, 0:1])
        neg_a = -jnp.exp(alog_ref[h])
        gc_col = _seg_cumsum(neg_a * _softplus(ba_c[:, 1:2] + dtb_ref[h]), 0, chunk)
        gc_row = _seg_cumsum(neg_a * _softplus(ba_r[1:2, :] + dtb_ref[h]), 1, chunk)
        lower, rhs, qk, q_dec = _gdn_chunk_local(qh, kh, vh, beta, gc_col, gc_row, causal, strict)
        heads.append(dict(lower=lower, rhs=rhs, qk=qk, q_dec=q_dec, k=kh, gc=gc_col, state=s_ref[h],
                          v_new=[], o_state=[]))
    for hd, t_inv in zip(heads, _tri_inv([hd["lower"] for hd in heads], chunk)):
        sol = _dot(t_inv, hd["rhs"])
        hd.update(u=sol[:, :dv], w=sol[:, dv:])

    for c in range(tt // chunk):
        sl = slice(c * chunk, (c + 1) * chunk)
        for hd in heads:
            g_last = hd["gc"][(c + 1) * chunk - 1:(c + 1) * chunk, :]
            vn = hd["u"][sl] - _dot(hd["w"][sl], hd["state"])
            hd["o_state"].append(_dot(hd["q_dec"][sl], hd["state"]))
            hd["state"] = hd["state"] * jnp.exp(g_last) + _dot_tn(hd["k"][sl] * jnp.exp(g_last - hd["gc"][sl]), vn)
            hd["v_new"].append(vn)

    for h, hd in enumerate(heads):
        s_ref[h] = hd["state"]
        o = jnp.concatenate(hd["o_state"], axis=0) + _dot(hd["qk"], jnp.concatenate(hd["v_new"], axis=0))
        o = _rms(o, nw_ref[...], NORM_EPS) * _silu(z_ref[0, :, h * dv:(h + 1) * dv])
        y_ref[0, :, h * dv:(h + 1) * dv] = o.astype(y_ref.dtype)


def _gdn_core(qkv, z, ba, conv_w, a_log, dt_bias, norm_w, *, heads, dk, dv):
    bsz, t, _ = z.shape
    hb = GDN_HEADS_PER_STEP
    nhb = heads // hb
    hw = hb * dk
    ba = ba.reshape(bsz, t, 2, heads)
    ba_col = ba.transpose(0, 3, 1, 2)
    ba_row = ba.transpose(0, 3, 2, 1)

    def col(off):
        return pl.BlockSpec((1, TIME_TILE, hw), lambda b, j, i: (b, i, off + j))

    def wcol(off):
        return pl.BlockSpec((CONV_TAPS, hw), lambda b, j, i: (0, off + j))

    per_head = pl.BlockSpec((hb, 1, 1), lambda b, j, i: (j, 0, 0))
    return pl.pallas_call(
        functools.partial(_gdn_kernel, hb=hb, dk=dk, dv=dv, chunk=GDN_CHUNK),
        grid=(bsz, nhb, t // TIME_TILE),
        in_specs=[col(0), col(nhb), col(2 * nhb), col(0),
                  pl.BlockSpec((1, hb, TIME_TILE, 2), lambda b, j, i: (b, j, i, 0)),
                  pl.BlockSpec((1, hb, 2, TIME_TILE), lambda b, j, i: (b, j, 0, i)),
                  wcol(0), wcol(nhb), wcol(2 * nhb), per_head, per_head,
                  pl.BlockSpec((1, dv), lambda b, j, i: (0, 0))],
        out_specs=col(0),
        out_shape=jax.ShapeDtypeStruct((bsz, t, heads * dv), BF16),
        scratch_shapes=[pltpu.VMEM((SUBLANES, hw), F32)] * 3 + [pltpu.VMEM((hb, dk, dv), F32)],
        compiler_params=_params(3, 48),
        name="gdn_core",
    )(qkv, qkv, qkv, z, ba_col, ba_row, conv_w, conv_w, conv_w,
      a_log.reshape(heads, 1, 1), dt_bias.reshape(heads, 1, 1), norm_w.reshape(1, dv))


def _gdn_mixer(x2d, g_in, g_out, shape, w_in, conv_w, a_log, dt_bias, norm_w, w_out):
    bsz, t, d = shape
    heads = a_log.shape[0]
    dv = norm_w.shape[0]
    n_qkv = conv_w.shape[1]
    dk = (n_qkv - heads * dv) // (2 * heads)
    w = w_in.astype(BF16)
    qkv, z, ba = _norm_matmul(
        x2d, g_in, [w[:, :n_qkv], w[:, n_qkv:n_qkv + heads * dv], w[:, n_qkv + heads * dv:]], "gdn_in")
    y = _gdn_core(qkv.reshape(bsz, t, -1), z.reshape(bsz, t, -1), ba.reshape(bsz, t, -1), conv_w, a_log, dt_bias,
                  norm_w, heads=heads, dk=dk, dv=dv)
    return _out_proj(y.reshape(bsz * t, -1), w_out.astype(BF16), g_out, x2d, "gdn_out")


def _rwkv_in_kernel(x_ref, g_ref, mu_ref, wr_ref, wk_ref, wv_ref, w0_ref, w1_ref, w2_ref, a0_ref, a1_ref, a2_ref,
                    g1_ref, g2_ref, r_ref, k_ref, v_ref, w_ref, a_ref, go_ref, prev_ref):
    @pl.when(pl.program_id(1) == 0)
    def _init():
        prev_ref[...] = jnp.zeros_like(prev_ref)

    hn = _rms(x_ref[0], g_ref[...], NORM_EPS)
    tm = hn.shape[0]
    ext = jnp.concatenate([prev_ref[...], hn], axis=0)
    xx = pltpu.roll(ext, 1, 0)[SUBLANES:, :] - hn
    prev_ref[...] = hn[tm - SUBLANES:, :]
    mu = mu_ref[...]

    def mix(i):
        return (hn + xx * mu[i:i + 1, :]).astype(BF16)

    r_ref[0] = jnp.dot(mix(0), wr_ref[...], preferred_element_type=F32)
    k_ref[0] = jnp.dot(mix(1), wk_ref[...], preferred_element_type=F32)
    v_ref[0] = jnp.dot(mix(2), wv_ref[...], preferred_element_type=F32)
    w_ref[0] = w0_ref[...] + _dot(jnp.tanh(jnp.dot(mix(3), w1_ref[...], preferred_element_type=F32)), w2_ref[...])
    a_ref[0] = a0_ref[...] + _dot(jnp.dot(mix(4), a1_ref[...], preferred_element_type=F32), a2_ref[...])
    go_ref[0] = _dot(_sigmoid(jnp.dot(mix(5), g1_ref[...], preferred_element_type=F32)), g2_ref[...])


def _rwkv_chunk_local(r, k, v, w_pre, a_pre, kk_w, ka_w, causal, strict, *, hd, chunk):
    tt, width = r.shape
    n_heads = width // hd
    lane = lax.broadcasted_iota(jnp.int32, (1, width), 1)
    head_of_lane = [(lane >= j * hd) & (lane < (j + 1) * hd) for j in range(n_heads)]

    def by_head(parts):
        out = parts[-1]
        for m, p in zip(head_of_lane[:-1], parts[:-1]):
            out = jnp.where(m, p, out)
        return out

    log_w = -jnp.exp(-_softplus(-w_pre) - 0.5)
    a = _sigmoid(a_pre)
    kk = k * kk_w
    kk = kk * lax.rsqrt(_head_sum(kk * kk, hd) + L2_EPS)
    k2 = k * (1.0 + (a - 1.0) * ka_w)
    cum = _seg_cumsum(log_w, 0, chunk)
    inv = jnp.exp(-cum)
    a_t = -kk * jnp.exp(cum - log_w)
    b_t = kk * a * inv
    k_t = k2 * inv
    r_t = r * jnp.exp(cum)

    bk = jnp.concatenate([b_t, k_t], axis=0)
    lows, rhs, y0_parts, p_rb = [], [], [], []
    for m in head_of_lane:
        ar = jnp.concatenate([jnp.where(m, a_t, 0.0), jnp.where(m, r_t, 0.0)], axis=0)
        g = _dot_nt(ar, bk)
        lows.append(jnp.where(strict, -g[:tt, :tt], 0.0))
        a_ak = jnp.where(strict, g[:tt, tt:], 0.0)
        p_rb.append(jnp.where(causal, g[tt:, :tt], 0.0))
        p_rk = jnp.where(causal, g[tt:, tt:], 0.0)
        rhs.append(jnp.concatenate([_dot(a_ak, v), a_t], axis=1))
        y0_parts.append(_dot(p_rk, v))
    return dict(lows=lows, rhs=rhs, y0=by_head(y0_parts), p_rb=p_rb, by_head=by_head,
                b_t=b_t, k_t=k_t, r_t=r_t, cum=cum, k2=k2, r=r, v=v)


def _head_sum(x, hd):
    lane = lax.broadcasted_iota(jnp.int32, (1, x.shape[1]), 1)
    tot = jnp.zeros_like(x)
    for j in range(x.shape[1] // hd):
        m = (lane >= j * hd) & (lane < (j + 1) * hd)
        tot = tot + jnp.where(m, jnp.sum(jnp.where(m, x, 0.0), axis=-1, keepdims=True), 0.0)
    return tot


def _rwkv_core_kernel(r_ref, k_ref, v_ref, w_ref, a_ref, g_ref, kk_ref, ka_ref, rk_ref, lnw_ref, lnb_ref,
                      y_ref, s_ref, *, hd, chunk):
    @pl.when(pl.program_id(2) == 0)
    def _init():
        s_ref[...] = jnp.zeros_like(s_ref)

    tt, width = r_ref.shape[1:]
    shift = chunk.bit_length() - 1
    rowi = lax.broadcasted_iota(jnp.int32, (tt, tt), 0)
    coli = lax.broadcasted_iota(jnp.int32, (tt, tt), 1)
    same_chunk = (rowi >> shift) == (coli >> shift)
    causal = same_chunk & (rowi >= coli)
    strict = same_chunk & (rowi > coli)
    srow = lax.broadcasted_iota(jnp.int32, (LANES, LANES), 0)
    scol = lax.broadcasted_iota(jnp.int32, (LANES, LANES), 1)
    same_head = None
    for j in range(LANES // hd):
        blk = (srow >= j * hd) & (srow < (j + 1) * hd) & (scol >= j * hd) & (scol < (j + 1) * hd)
        same_head = blk if same_head is None else (same_head | blk)

    groups = []
    for grp in range(width // LANES):
        ls = slice(grp * LANES, (grp + 1) * LANES)
        loc = _rwkv_chunk_local(r_ref[0, :, ls], k_ref[0, :, ls], v_ref[0, :, ls], w_ref[0, :, ls], a_ref[0, :, ls],
                                kk_ref[:, ls], ka_ref[:, ls], causal, strict, hd=hd, chunk=chunk)
        loc.update(state=s_ref[grp], us=[], rs=[])
        groups.append(loc)
    t_invs = _tri_inv([low for gp in groups for low in gp["lows"]], chunk)
    for gi, gp in enumerate(groups):
        n_h = len(gp["lows"])
        sols = [_dot(t_inv, rhs) for t_inv, rhs in zip(t_invs[gi * n_h:(gi + 1) * n_h], gp["rhs"])]
        gp.update(u0=gp["by_head"]([s[:, :LANES] for s in sols]), w_t=gp["by_head"]([s[:, LANES:] for s in sols]))

    for c in range(tt // chunk):
        sl = slice(c * chunk, (c + 1) * chunk)
        for gp in groups:
            u_c = gp["u0"][sl] + _dot_nt(gp["w_t"][sl], gp["state"])
            gp["rs"].append(_dot_nt(gp["r_t"][sl], gp["state"]))
            decay_end = jnp.exp(gp["cum"][(c + 1) * chunk - 1:(c + 1) * chunk, :])
            upd = gp["state"] + _dot_tn(u_c, gp["b_t"][sl]) + _dot_tn(gp["v"][sl], gp["k_t"][sl])
            gp["state"] = jnp.where(same_head, upd, 0.0) * decay_end
            gp["us"].append(u_c)

    for grp, gp in enumerate(groups):
        ls = slice(grp * LANES, (grp + 1) * LANES)
        s_ref[grp] = gp["state"]
        u = jnp.concatenate(gp["us"], axis=0)
        y = jnp.concatenate(gp["rs"], axis=0) + gp["y0"] + gp["by_head"]([_dot(p, u) for p in gp["p_rb"]])
        mean = _head_sum(y, hd) * (1.0 / hd)
        yc = y - mean
        var = _head_sum(yc * yc, hd) * (1.0 / hd)
        y = yc * lax.rsqrt(var + RWKV_GN_EPS) * lnw_ref[:, ls] + lnb_ref[:, ls]
        bonus = _head_sum(gp["r"] * gp["k2"] * rk_ref[:, ls], hd) * gp["v"]
        y_ref[0, :, ls] = ((y + bonus) * g_ref[0, :, ls]).astype(y_ref.dtype)


def _rwkv_mixer(x2d, g_in, g_out, shape, mu, w_rkv, w0, w1, w2, a0, a1, a2, g1, g2, k_k, k_a, r_k, ln_w, ln_b,
                w_out):
    bsz, t, d = shape
    hd = r_k.shape[1]
    x3 = x2d.reshape(bsz, t, d)
    wb = w_rkv.astype(BF16)
    tile = pl.BlockSpec((1, ROW_TILE, d), lambda b, i: (b, i, 0))
    vec = _resident((1, d))
    r, k, v, w, a, g = pl.pallas_call(
        _rwkv_in_kernel,
        grid=(bsz, t // ROW_TILE),
        in_specs=[tile, vec, _resident(mu.shape), _resident((d, d)), _resident((d, d)), _resident((d, d)),
                  vec, _resident(w1.shape), _resident(w2.shape), vec, _resident(a1.shape), _resident(a2.shape),
                  _resident(g1.shape), _resident(g2.shape)],
        out_specs=[tile] * 6,
        out_shape=[jax.ShapeDtypeStruct((bsz, t, d), F32)] * 6,
        scratch_shapes=[pltpu.VMEM((SUBLANES, d), F32)],
        compiler_params=_params(2, 48),
        name="rwkv_in",
    )(x3, g_in.reshape(1, d), mu, wb[0], wb[1], wb[2], w0.reshape(1, d), w1.astype(BF16), w2.astype(BF16),
      a0.reshape(1, d), a1.astype(BF16), a2.astype(BF16), g1.astype(BF16), g2.astype(BF16))

    width = RWKV_WIDTH
    blk = pl.BlockSpec((1, TIME_TILE, width), lambda b, j, i: (b, i, j))
    pvec = pl.BlockSpec((1, width), lambda b, j, i: (0, j))
    y = pl.pallas_call(
        functools.partial(_rwkv_core_kernel, hd=hd, chunk=RWKV_CHUNK),
        grid=(bsz, d // width, t // TIME_TILE),
        in_specs=[blk] * 6 + [pvec] * 5,
        out_specs=blk,
        out_shape=jax.ShapeDtypeStruct((bsz, t, d), BF16),
        scratch_shapes=[pltpu.VMEM((width // LANES, LANES, LANES), F32)],
        compiler_params=_params(3, 48),
        name="rwkv_core",
    )(r, k, v, w, a, g, k_k.reshape(1, d), k_a.reshape(1, d), r_k.reshape(1, d), ln_w.reshape(1, d),
      ln_b.reshape(1, d))
    return _out_proj(y.reshape(bsz * t, d), w_out.astype(BF16), g_out, x2d, "rwkv_out")


def _ssd_kernel(z_ref, x_ref, b_ref, c_ref, dtc_ref, dtr_ref, cwx_ref, cbx_ref, cwb_ref, cbb_ref, cwc_ref, cbc_ref,
                dtbc_ref, dtbr_ref, alc_ref, alr_ref, dskip_ref, nw_ref, y_ref, tx_ref, tb_ref, tc_ref, h_ref,
                *, heads, hp, chunk):
    @pl.when(pl.program_id(2) == 0)
    def _init():
        tx_ref[...] = jnp.zeros_like(tx_ref)
        tb_ref[...] = jnp.zeros_like(tb_ref)
        tc_ref[...] = jnp.zeros_like(tc_ref)
        h_ref[...] = jnp.zeros_like(h_ref)

    xs = _silu(_causal_conv(x_ref[0], tx_ref, cwx_ref[...]) + cbx_ref[...])
    bm = _silu(_causal_conv(b_ref[0], tb_ref, cwb_ref[...]) + cbb_ref[...])
    cm = _silu(_causal_conv(c_ref[0], tc_ref, cwc_ref[...]) + cbc_ref[...])
    tt, width = xs.shape
    dt_c = _softplus(dtc_ref[0, 0] + dtbc_ref[0])
    dt_r = _softplus(dtr_ref[0, 0] + dtbr_ref[0])
    da_c = dt_c * -jnp.exp(alc_ref[0])
    da_r = dt_r * -jnp.exp(alr_ref[0])
    acs_c = _seg_cumsum(da_c, 0, chunk)
    acs_r = _seg_cumsum(da_r, 1, chunk)

    lane = lax.broadcasted_iota(jnp.int32, (1, width), 1)
    head_of_lane = [(lane >= e * hp) & (lane < (e + 1) * hp) for e in range(heads)]

    def expand(cols):
        out = jnp.broadcast_to(cols[:, heads - 1:heads], (cols.shape[0], width))
        for e in range(heads - 1):
            out = jnp.where(head_of_lane[e], cols[:, e:e + 1], out)
        return out

    rowi = lax.broadcasted_iota(jnp.int32, (chunk, chunk), 0)
    coli = lax.broadcasted_iota(jnp.int32, (chunk, chunk), 1)
    causal = rowi >= coli
    pair_lo = lax.broadcasted_iota(jnp.int32, (1, 2 * hp), 1) < hp

    state = h_ref[...]
    ys = []
    for c in range(tt // chunk):
        sl = slice(c * chunk, (c + 1) * chunk)
        xc, bc, cc = xs[sl], bm[sl], cm[sl]
        ac_c = acs_c[sl]
        ac_r = acs_r[:, sl]
        tot = ac_c[chunk - 1:chunk, :]
        xdt = xc * expand(dt_c[sl])
        cb = _dot_nt(cc, bc)
        y_pairs = []
        for p in range(heads // 2):
            xp = xdt[:, 2 * p * hp:2 * (p + 1) * hp]
            halves = []
            for e in (2 * p, 2 * p + 1):
                seg = jnp.where(causal, ac_c[:, e:e + 1] - ac_r[e:e + 1, :], -jnp.inf)
                halves.append(_dot(cb * jnp.exp(seg), xp))
            y_pairs.append(jnp.where(pair_lo, halves[0], halves[1]))
        y_diag = jnp.concatenate(y_pairs, axis=1)
        y_off = _dot(cc, state) * expand(jnp.exp(ac_c))
        new = _dot_tn(bc, xdt * expand(jnp.exp(tot - ac_c)))
        state = state * expand(jnp.exp(tot)) + new
        ys.append(y_diag + y_off + xc * dskip_ref[0])
    h_ref[...] = state

    y = jnp.concatenate(ys, axis=0) * _silu(z_ref[0])
    y_ref[0] = _rms(y, nw_ref[0], SSD_NORM_EPS).astype(y_ref.dtype)


def _ssd_mixer(x2d, g_in, g_out, shape, w_in, conv_w, conv_b, dt_bias, a_log, d_skip, norm_w, w_out):
    bsz, t, d = shape
    n_heads = a_log.shape[0]
    inner = norm_w.shape[0]
    hp = inner // n_heads
    n_xbc = conv_w.shape[1]
    groups = 4
    ns = (n_xbc - inner) // (2 * groups)
    gh = n_heads // groups
    gw = inner // groups
    w = w_in.astype(BF16)
    o_b, o_c, o_dt = 2 * inner, 2 * inner + groups * ns, inner + n_xbc
    z, xs, bm, cm, dt = _norm_matmul(
        x2d, g_in, [w[:, :inner], w[:, inner:o_b], w[:, o_b:o_c], w[:, o_c:o_dt], w[:, o_dt:]], "ssd_in")
    dt = dt.reshape(bsz, t, groups, gh)
    dt_col = dt.transpose(0, 2, 1, 3)
    dt_row = dt.transpose(0, 2, 3, 1)

    def col(width):
        return pl.BlockSpec((1, TIME_TILE, width), lambda b, j, i: (b, i, j))

    def wspec(rows, width, off):
        return pl.BlockSpec((rows, width), lambda b, j, i: (0, off + j))

    def grp(shape2):
        return pl.BlockSpec((1,) + shape2, lambda b, j, i: (j, 0, 0))

    cw_x, cw_b, cw_c = conv_w[:, :inner], conv_w[:, inner:inner + groups * ns], conv_w[:, inner + groups * ns:]
    cb = conv_b.reshape(1, n_xbc)
    cb_x, cb_b, cb_c = cb[:, :inner], cb[:, inner:inner + groups * ns], cb[:, inner + groups * ns:]
    y = pl.pallas_call(
        functools.partial(_ssd_kernel, heads=gh, hp=hp, chunk=SSD_CHUNK),
        grid=(bsz, groups, t // TIME_TILE),
        in_specs=[col(gw), col(gw), col(ns), col(ns),
                  pl.BlockSpec((1, 1, TIME_TILE, gh), lambda b, j, i: (b, j, i, 0)),
                  pl.BlockSpec((1, 1, gh, TIME_TILE), lambda b, j, i: (b, j, 0, i)),
                  wspec(CONV_TAPS, gw, 0), wspec(1, gw, 0), wspec(CONV_TAPS, ns, 0), wspec(1, ns, 0),
                  wspec(CONV_TAPS, ns, 0), wspec(1, ns, 0),
                  grp((1, gh)), grp((gh, 1)), grp((1, gh)), grp((gh, 1)), grp((1, gw)), grp((1, gw))],
        out_specs=col(gw),
        out_shape=jax.ShapeDtypeStruct((bsz, t, inner), BF16),
        scratch_shapes=[pltpu.VMEM((SUBLANES, gw), F32), pltpu.VMEM((SUBLANES, ns), F32),
                        pltpu.VMEM((SUBLANES, ns), F32), pltpu.VMEM((ns, gw), F32)],
        compiler_params=_params(3, 32),
        name="ssd_core",
    )(z.reshape(bsz, t, inner), xs.reshape(bsz, t, inner), bm.reshape(bsz, t, -1), cm.reshape(bsz, t, -1),
      dt_col, dt_row, cw_x, cb_x, cw_b, cb_b, cw_c, cb_c,
      dt_bias.reshape(groups, 1, gh), dt_bias.reshape(groups, gh, 1),
      a_log.reshape(groups, 1, gh), a_log.reshape(groups, gh, 1),
      jnp.repeat(d_skip, hp).reshape(groups, 1, gw), norm_w.reshape(groups, 1, gw))
    return _out_proj(y.reshape(bsz * t, inner), w_out.astype(BF16), g_out, x2d, "ssd_out")


def _lru_kernel(gate_ref, xb_ref, cw_ref, cb_ref, wg_ref, bg_ref, lam_ref, y_ref, tail_ref, h_ref):
    @pl.when(pl.program_id(2) == 0)
    def _init():
        tail_ref[...] = jnp.zeros_like(tail_ref)
        h_ref[...] = jnp.zeros_like(h_ref)

    u = _causal_conv(xb_ref[0], tail_ref, cw_ref[...]) + cb_ref[...]
    tt, bw = u.shape
    gates = _dot(u, wg_ref[0])
    r = _sigmoid(gates[:, :bw] + bg_ref[0:1, :])
    i = _sigmoid(gates[:, bw:] + bg_ref[1:2, :])
    log_a = -LRU_C * r * _softplus(-lam_ref[...])
    a = jnp.exp(log_a)
    bterm = jnp.sqrt(1.0 - jnp.exp(2.0 * log_a)) * (i * u)
    a_cum, h_loc = _lin_scan(a, bterm)
    h = h_loc + a_cum * h_ref[...]
    h_ref[...] = h[tt - 1:tt, :]
    y_ref[0] = (_gelu_tanh(gate_ref[0]) * h).astype(y_ref.dtype)


def _lru_mixer(x2d, g_in, g_out, shape, w_in, conv_w, conv_b, w_gates, b_gates, lam, w_out):
    bsz, t, d = shape
    width = lam.shape[0]
    blocks, bw, _ = w_gates.shape
    w = w_in.astype(BF16)
    gate, xb = _norm_matmul(x2d, g_in, [w[:, :width], w[:, width:]], "lru_in")
    blk = pl.BlockSpec((1, TIME_TILE, bw), lambda b, j, i: (b, i, j))

    def pvec(rows):
        return pl.BlockSpec((rows, bw), lambda b, j, i: (0, j))

    y = pl.pallas_call(
        _lru_kernel,
        grid=(bsz, blocks, t // TIME_TILE),
        in_specs=[blk, blk, pvec(CONV_TAPS), pvec(1),
                  pl.BlockSpec((1, bw, 2 * bw), lambda b, j, i: (j, 0, 0)), pvec(2), pvec(1)],
        out_specs=blk,
        out_shape=jax.ShapeDtypeStruct((bsz, t, width), BF16),
        scratch_shapes=[pltpu.VMEM((SUBLANES, bw), F32), pltpu.VMEM((1, bw), F32)],
        compiler_params=_params(3, 32),
        name="lru_core",
    )(gate.reshape(bsz, t, width), xb.reshape(bsz, t, width), conv_w, conv_b.reshape(1, width),
      w_gates.astype(BF16), b_gates, lam.reshape(1, width))
    return _out_proj(y.reshape(bsz * t, width), w_out.astype(BF16), g_out, x2d, "lru_out")


def kernel(x, norm_g, ffn_w_in, ffn_w_out, gdn_w_in, gdn_conv_w, gdn_a_log, gdn_dt_bias, gdn_norm_w, gdn_w_out, rwkv_mu, rwkv_w_rkv, rwkv_w0, rwkv_w1, rwkv_w2, rwkv_a0, rwkv_a1, rwkv_a2, rwkv_g1, rwkv_g2, rwkv_k_k, rwkv_k_a, rwkv_r_k, rwkv_ln_w, rwkv_ln_b, rwkv_w_out, ssd_w_in, ssd_conv_w, ssd_conv_b, ssd_dt_bias, ssd_a_log, ssd_d, ssd_norm_w, ssd_w_out, lru_w_in, lru_conv_w, lru_conv_b, lru_w_gates, lru_b_gates, lru_lam, lru_w_out):
    shape = x.shape
    bsz, t, d = shape
    depth = norm_g.shape[0]
    n_mixers = 4
    x2d = x.reshape(bsz * t, d)
    for i in range(depth):
        m, j = i % n_mixers, i // n_mixers
        g = norm_g[i]
        x2d = _ffn(x2d, g[0], g[1], ffn_w_in[i, 0].astype(BF16), ffn_w_out[i, 0].astype(BF16))
        if m == 0:
            x2d = _gdn_mixer(x2d, g[2], g[3], shape, gdn_w_in[j], gdn_conv_w[j], gdn_a_log[j], gdn_dt_bias[j],
                             gdn_norm_w[j], gdn_w_out[j])
        elif m == 1:
            x2d = _rwkv_mixer(x2d, g[2], g[3], shape, rwkv_mu[j], rwkv_w_rkv[j], rwkv_w0[j], rwkv_w1[j],
                              rwkv_w2[j], rwkv_a0[j], rwkv_a1[j], rwkv_a2[j], rwkv_g1[j], rwkv_g2[j], rwkv_k_k[j],
                              rwkv_k_a[j], rwkv_r_k[j], rwkv_ln_w[j], rwkv_ln_b[j], rwkv_w_out[j])
        elif m == 2:
            x2d = _ssd_mixer(x2d, g[2], g[3], shape, ssd_w_in[j], ssd_conv_w[j], ssd_conv_b[j], ssd_dt_bias[j],
                             ssd_a_log[j], ssd_d[j], ssd_norm_w[j], ssd_w_out[j])
        else:
            x2d = _lru_mixer(x2d, g[2], g[3], shape, lru_w_in[j], lru_conv_w[j], lru_conv_b[j], lru_w_gates[j],
                             lru_b_gates[j], lru_lam[j], lru_w_out[j])
        x2d = _ffn(x2d, g[4], g[5], ffn_w_in[i, 1].astype(BF16), ffn_w_out[i, 1].astype(BF16))
    return x2d.reshape(shape)
```

```python
import functools

import jax
import jax.numpy as jnp
from jax import lax
from jax.experimental import pallas as pl
from jax.experimental.pallas import tpu as pltpu

F32 = jnp.float32
BF16 = jnp.bfloat16

V7X_VMEM_BYTES = 64 * 1024 * 1024
SUBLANES = 8
LANES = 128
MXU_WIDTH = 256

NORM_EPS = 1e-6
L2_EPS = 1e-6
SSD_NORM_EPS = 1e-5
RWKV_GN_EPS = 64e-5
LRU_C = 8.0

ROW_TILE = 512
TIME_TILE = 256
GDN_CHUNK = 64
RWKV_CHUNK = 64
RWKV_WIDTH = 512
GDN_HEADS_PER_STEP = 4
SSD_CHUNK = 128
SSD_TIME_TILE = 512
SSD_GROUPS = 4
CONV_TAPS = 4


def _params(n_axes, vmem_mib):
    return pltpu.CompilerParams(
        dimension_semantics=("arbitrary",) * n_axes,
        vmem_limit_bytes=vmem_mib * 1024 * 1024,
    )


def _resident(shape):
    zeros = (0,) * len(shape)
    return pl.BlockSpec(shape, lambda *_: zeros, pipeline_mode=pl.Buffered(1))


def _rms(x, g, eps):
    return x * lax.rsqrt(jnp.mean(x * x, axis=-1, keepdims=True) + eps) * g


def _sigmoid(x):
    return 1.0 / (1.0 + jnp.exp(-x))


def _silu(x):
    return x * _sigmoid(x)


def _softplus(x):
    return jnp.maximum(x, 0.0) + jnp.log1p(jnp.exp(-jnp.abs(x)))


def _gelu_tanh(x):
    return 0.5 * x * (1.0 + jnp.tanh(0.7978845608028654 * (x + 0.044715 * (x * x * x))))


def _dot(a, b):
    return jnp.dot(a.astype(BF16), b.astype(BF16), preferred_element_type=F32)


def _dot_nt(a, b):
    return lax.dot_general(a.astype(BF16), b.astype(BF16), (((1,), (1,)), ((), ())),
                           preferred_element_type=F32)


def _dot_tn(a, b):
    return lax.dot_general(a.astype(BF16), b.astype(BF16), (((0,), (0,)), ((), ())),
                           preferred_element_type=F32)


def _seg_cumsum(x, axis, seg):
    pos = lax.broadcasted_iota(jnp.int32, x.shape, axis) & (seg - 1)
    s = 1
    while s < seg:
        x = x + jnp.where(pos >= s, pltpu.roll(x, s, axis), 0.0)
        s *= 2
    return x


def _causal_conv(pre, tail_ref, w):
    tt = pre.shape[0]
    ext = jnp.concatenate([tail_ref[...], pre], axis=0)
    y = pre * w[CONV_TAPS - 1:CONV_TAPS, :]
    for k in range(1, CONV_TAPS):
        y = y + pltpu.roll(ext, k, 0)[SUBLANES:, :] * w[CONV_TAPS - 1 - k:CONV_TAPS - k, :]
    tail_ref[...] = pre[tt - SUBLANES:, :]
    return y


def _tri_inv(lows, block):
    n = lows[0].shape[0]
    n_sys = n // block
    row = lax.broadcasted_iota(jnp.int32, (n, n), 0)
    col = lax.broadcasted_iota(jnp.int32, (n, n), 1)
    on_diag = (row >> (block.bit_length() - 1)) == (col >> (block.bit_length() - 1))

    def to_diag(m_l):
        return jnp.where(on_diag, jnp.concatenate([m_l] * n_sys, axis=0), 0.0)

    row_l = lax.broadcasted_iota(jnp.int32, (block, n), 0)
    col_l = lax.broadcasted_iota(jnp.int32, (block, n), 1) & (block - 1)
    eye_l = jnp.where(row_l == col_l, 1.0, 0.0)
    pair_l = (row_l >> 1) == (col_l >> 1)
    t_ls = []
    for low in lows:
        low_l = low[:block]
        for s in range(1, n_sys):
            low_l = low_l + low[s * block:(s + 1) * block]
        t_ls.append(eye_l - jnp.where(pair_l, low_l, 0.0))
    b, sh = 2, 1
    while b < block:
        sub = ((row >> (sh + 1)) == (col >> (sh + 1))) & (((row >> sh) & 1) == 1) & (((col >> sh) & 1) == 0)
        xs = [_dot(t_l, jnp.where(sub, low, 0.0)) for t_l, low in zip(t_ls, lows)]
        t_ls = [t_l - _dot(x, to_diag(t_l)) for t_l, x in zip(t_ls, xs)]
        b, sh = 2 * b, sh + 1
    return [to_diag(t_l) for t_l in t_ls]


def _ffn_half_step(x, g0, g1, win_ref, wout_ref):
    d_ff = wout_ref.shape[0]
    mid = (d_ff // MXU_WIDTH // 2) * MXU_WIDTH
    xn = _rms(x, g0, NORM_EPS).astype(BF16)
    acc = None
    for lo, hi in ((0, mid), (mid, d_ff)) if 0 < mid < d_ff else ((0, d_ff),):
        gate = jnp.dot(xn, win_ref[:, lo:hi], preferred_element_type=F32)
        up = jnp.dot(xn, win_ref[:, d_ff + lo:d_ff + hi], preferred_element_type=F32)
        act = (_silu(gate) * up).astype(BF16)
        part = jnp.dot(act, wout_ref[lo:hi, :], preferred_element_type=F32)
        acc = part if acc is None else acc + part
    return x + 0.5 * _rms(acc, g1, NORM_EPS)


def _ffn_kernel(x_ref, g0_ref, g1_ref, win_ref, wout_ref, o_ref):
    o_ref[...] = _ffn_half_step(x_ref[...], g0_ref[...], g1_ref[...], win_ref, wout_ref)


def _ffn(x2d, g0, g1, w_in, w_out):
    n, d = x2d.shape
    row = pl.BlockSpec((ROW_TILE, d), lambda i: (i, 0))
    return pl.pallas_call(
        _ffn_kernel,
        grid=(n // ROW_TILE,),
        in_specs=[row, _resident((1, d)), _resident((1, d)), _resident(w_in.shape), _resident(w_out.shape)],
        out_specs=row,
        out_shape=jax.ShapeDtypeStruct((n, d), F32),
        compiler_params=_params(1, 56),
        name="ffn",
    )(x2d, g0.reshape(1, d), g1.reshape(1, d), w_in, w_out)


def _mix_out_ffn_kernel(y_ref, wo_ref, go_ref, x_ref, g0_ref, g1_ref, win_ref, wout_ref, o_ref):
    h = jnp.dot(y_ref[...], wo_ref[...], preferred_element_type=F32)
    x = x_ref[...] + _rms(h, go_ref[...], NORM_EPS)
    o_ref[...] = _ffn_half_step(x, g0_ref[...], g1_ref[...], win_ref, wout_ref)


def _mix_out_ffn(y2d, w_o, g_o, x2d, g0, g1, w_in, w_out):
    n, d = x2d.shape
    k = y2d.shape[1]
    row = pl.BlockSpec((ROW_TILE, d), lambda i: (i, 0))
    vec = _resident((1, d))
    return pl.pallas_call(
        _mix_out_ffn_kernel,
        grid=(n // ROW_TILE,),
        in_specs=[pl.BlockSpec((ROW_TILE, k), lambda i: (i, 0)), _resident(w_o.shape), vec, row, vec, vec,
                  _resident(w_in.shape), _resident(w_out.shape)],
        out_specs=row,
        out_shape=jax.ShapeDtypeStruct((n, d), F32),
        compiler_params=_params(1, 56),
        name="mix_out_ffn",
    )(y2d, w_o, g_o.reshape(1, d), x2d, g0.reshape(1, d), g1.reshape(1, d), w_in, w_out)


def _norm_matmul_kernel(x_ref, g_ref, w_ref, *o_refs, bounds):
    xn = _rms(x_ref[...], g_ref[...], NORM_EPS).astype(BF16)
    for o_ref, lo, hi in zip(o_refs, bounds[:-1], bounds[1:]):
        o_ref[...] = jnp.dot(xn, w_ref[:, lo:hi], preferred_element_type=F32).astype(o_ref.dtype)


def _norm_matmul(x2d, g, w, bounds, name):
    n, d = x2d.shape
    widths = [hi - lo for lo, hi in zip(bounds[:-1], bounds[1:])]
    return pl.pallas_call(
        functools.partial(_norm_matmul_kernel, bounds=tuple(bounds)),
        grid=(n // ROW_TILE,),
        in_specs=[pl.BlockSpec((ROW_TILE, d), lambda i: (i, 0)), _resident((1, d)), _resident(w.shape)],
        out_specs=[pl.BlockSpec((ROW_TILE, wd), lambda i: (i, 0)) for wd in widths],
        out_shape=[jax.ShapeDtypeStruct((n, wd), F32) for wd in widths],
        compiler_params=_params(1, 48),
        name=name,
    )(x2d, g.reshape(1, d), w)


def _gdn_chunk_local(q, k, v, beta, gc_col, gc_row, causal, strict):
    decay = jnp.exp(jnp.where(causal, gc_col - gc_row, -jnp.inf))
    lower = jnp.where(strict, _dot_nt(k, k) * decay * beta, 0.0)
    eg = jnp.exp(gc_col)
    rhs = jnp.concatenate([v * beta, k * (beta * eg)], axis=1)
    return lower, rhs, _dot_nt(q, k) * decay, q * eg


def _gdn_kernel(q_ref, k_ref, v_ref, z_ref, bac_ref, bar_ref, cwq_ref, cwk_ref, cwv_ref, alog_ref, dtb_ref,
                nw_ref, y_ref, tq_ref, tk_ref, tv_ref, s_ref, *, hb, dk, dv, chunk):
    @pl.when(pl.program_id(2) == 0)
    def _init():
        tq_ref[...] = jnp.zeros_like(tq_ref)
        tk_ref[...] = jnp.zeros_like(tk_ref)
        tv_ref[...] = jnp.zeros_like(tv_ref)
        s_ref[...] = jnp.zeros_like(s_ref)

    q = _silu(_causal_conv(q_ref[0], tq_ref, cwq_ref[...]))
    k = _silu(_causal_conv(k_ref[0], tk_ref, cwk_ref[...]))
    v = _silu(_causal_conv(v_ref[0], tv_ref, cwv_ref[...]))
    tt = q.shape[0]
    shift = chunk.bit_length() - 1
    row = lax.broadcasted_iota(jnp.int32, (tt, tt), 0)
    col = lax.broadcasted_iota(jnp.int32, (tt, tt), 1)
    same_chunk = (row >> shift) == (col >> shift)
    causal = same_chunk & (row >= col)
    strict = same_chunk & (row > col)
    heads = []
    for h in range(hb):
        qh = q[:, h * dk:(h + 1) * dk]
        kh = k[:, h * dk:(h + 1) * dk]
        vh = v[:, h * dv:(h + 1) * dv]
        qh = qh * (lax.rsqrt(jnp.sum(qh * qh, axis=-1, keepdims=True) + L2_EPS) * dk ** -0.5)
        kh = kh * lax.rsqrt(jnp.sum(kh * kh, axis=-1, keepdims=True) + L2_EPS)
        ba_c = bac_ref[0, h]
        ba_r = bar_ref[0, h]
        beta = _sigmoid(ba_c[:, 0:1])
        neg_a = -jnp.exp(alog_ref[h])
        gc_col = _seg_cumsum(neg_a * _softplus(ba_c[:, 1:2] + dtb_ref[h]), 0, chunk)
        gc_row = _seg_cumsum(neg_a * _softplus(ba_r[1:2, :] + dtb_ref[h]), 1, chunk)
        lower, rhs, qk, q_dec = _gdn_chunk_local(qh, kh, vh, beta, gc_col, gc_row, causal, strict)
        heads.append(dict(lower=lower, rhs=rhs, qk=qk, q_dec=q_dec, k=kh, gc=gc_col, state=s_ref[h],
                          v_new=[], o_state=[]))
    for hd, t_inv in zip(heads, _tri_inv([hd["lower"] for hd in heads], chunk)):
        sol = _dot(t_inv, hd["rhs"])
        hd.update(u=sol[:, :dv], w=sol[:, dv:])

    for c in range(tt // chunk):
        sl = slice(c * chunk, (c + 1) * chunk)
        for hd in heads:
            g_last = hd["gc"][(c + 1) * chunk - 1:(c + 1) * chunk, :]
            vn = hd["u"][sl] - _dot(hd["w"][sl], hd["state"])
            hd["o_state"].append(_dot(hd["q_dec"][sl], hd["state"]))
            hd["state"] = hd["state"] * jnp.exp(g_last) + _dot_tn(hd["k"][sl] * jnp.exp(g_last - hd["gc"][sl]), vn)
            hd["v_new"].append(vn)

    for h, hd in enumerate(heads):
        s_ref[h] = hd["state"]
        o = jnp.concatenate(hd["o_state"], axis=0) + _dot(hd["qk"], jnp.concatenate(hd["v_new"], axis=0))
        o = _rms(o, nw_ref[...], NORM_EPS) * _silu(z_ref[0, :, h * dv:(h + 1) * dv])
        y_ref[0, :, h * dv:(h + 1) * dv] = o.astype(y_ref.dtype)


def _gdn_core(qkv, z, ba, conv_w, a_log, dt_bias, norm_w, *, heads, dk, dv):
    bsz, t, _ = z.shape
    hb = GDN_HEADS_PER_STEP
    nhb = heads // hb
    hw = hb * dk
    ba = ba.reshape(bsz, t, 2, heads)
    ba_col = ba.transpose(0, 3, 1, 2)
    ba_row = ba.transpose(0, 3, 2, 1)

    def col(off):
        return pl.BlockSpec((1, TIME_TILE, hw), lambda b, j, i: (b, i, off + j))

    def wcol(off):
        return pl.BlockSpec((CONV_TAPS, hw), lambda b, j, i: (0, off + j))

    per_head = pl.BlockSpec((hb, 1, 1), lambda b, j, i: (j, 0, 0))
    return pl.pallas_call(
        functools.partial(_gdn_kernel, hb=hb, dk=dk, dv=dv, chunk=GDN_CHUNK),
        grid=(bsz, nhb, t // TIME_TILE),
        in_specs=[col(0), col(nhb), col(2 * nhb), col(0),
                  pl.BlockSpec((1, hb, TIME_TILE, 2), lambda b, j, i: (b, j, i, 0)),
                  pl.BlockSpec((1, hb, 2, TIME_TILE), lambda b, j, i: (b, j, 0, i)),
                  wcol(0), wcol(nhb), wcol(2 * nhb), per_head, per_head,
                  pl.BlockSpec((1, dv), lambda b, j, i: (0, 0))],
        out_specs=col(0),
        out_shape=jax.ShapeDtypeStruct((bsz, t, heads * dv), BF16),
        scratch_shapes=[pltpu.VMEM((SUBLANES, hw), F32)] * 3 + [pltpu.VMEM((hb, dk, dv), F32)],
        compiler_params=_params(3, 48),
        name="gdn_core",
    )(qkv, qkv, qkv, z, ba_col, ba_row, conv_w, conv_w, conv_w, a_log.reshape(heads, 1, 1), dt_bias.reshape(heads, 1, 1), norm_w.reshape(1, dv))


def _gdn_mixer(x2d, g_in, shape, w_in, conv_w, a_log, dt_bias, norm_w):
    bsz, t, d = shape
    heads = a_log.shape[0]
    dv = norm_w.shape[0]
    n_qkv = conv_w.shape[1]
    dk = (n_qkv - heads * dv) // (2 * heads)
    qkv, z, ba = _norm_matmul(x2d, g_in, w_in.astype(BF16), [0, n_qkv, n_qkv + heads * dv, w_in.shape[1]], "gdn_in")
    y = _gdn_core(qkv.reshape(bsz, t, -1), z.reshape(bsz, t, -1), ba.reshape(bsz, t, -1), conv_w, a_log, dt_bias,
                  norm_w, heads=heads, dk=dk, dv=dv)
    return y.reshape(bsz * t, -1)


def _rwkv_in_kernel(x_ref, g_ref, mu_ref, wrkv_ref, w0_ref, w1_ref, w2_ref, a0_ref, a1_ref, a2_ref,
                    g1_ref, g2_ref, r_ref, k_ref, v_ref, w_ref, a_ref, go_ref, prev_ref):
    @pl.when(pl.program_id(1) == 0)
    def _init():
        prev_ref[...] = jnp.zeros_like(prev_ref)

    hn = _rms(x_ref[0], g_ref[...], NORM_EPS)
    tm = hn.shape[0]
    ext = jnp.concatenate([prev_ref[...], hn], axis=0)
    xx = pltpu.roll(ext, 1, 0)[SUBLANES:, :] - hn
    prev_ref[...] = hn[tm - SUBLANES:, :]
    mu = mu_ref[...]

    def mix(i):
        return (hn + xx * mu[i:i + 1, :]).astype(BF16)

    r_ref[0] = jnp.dot(mix(0), wrkv_ref[0], preferred_element_type=F32)
    k_ref[0] = jnp.dot(mix(1), wrkv_ref[1], preferred_element_type=F32)
    v_ref[0] = jnp.dot(mix(2), wrkv_ref[2], preferred_element_type=F32)
    w_ref[0] = w0_ref[...] + _dot(jnp.tanh(jnp.dot(mix(3), w1_ref[...], preferred_element_type=F32)), w2_ref[...])
    a_ref[0] = a0_ref[...] + _dot(jnp.dot(mix(4), a1_ref[...], preferred_element_type=F32), a2_ref[...])
    go_ref[0] = _dot(_sigmoid(jnp.dot(mix(5), g1_ref[...], preferred_element_type=F32)), g2_ref[...])


def _rwkv_chunk_local(r, k, v, w_pre, a_pre, kk_w, ka_w, causal, strict, *, hd, chunk):
    tt, width = r.shape
    n_heads = width // hd
    lane = lax.broadcasted_iota(jnp.int32, (1, width), 1)
    head_of_lane = [(lane >= j * hd) & (lane < (j + 1) * hd) for j in range(n_heads)]

    def by_head(parts):
        out = parts[-1]
        for m, p in zip(head_of_lane[:-1], parts[:-1]):
            out = jnp.where(m, p, out)
        return out

    log_w = -jnp.exp(-_softplus(-w_pre) - 0.5)
    a = _sigmoid(a_pre)
    kk = k * kk_w
    kk = kk * lax.rsqrt(_head_sum(kk * kk, hd) + L2_EPS)
    k2 = k * (1.0 + (a - 1.0) * ka_w)
    cum = _seg_cumsum(log_w, 0, chunk)
    inv = jnp.exp(-cum)
    a_t = -kk * jnp.exp(cum - log_w)
    b_t = kk * a * inv
    k_t = k2 * inv
    r_t = r * jnp.exp(cum)

    bk = jnp.concatenate([b_t, k_t], axis=0)
    lows, rhs, y0_parts, p_rb = [], [], [], []
    for m in head_of_lane:
        ar = jnp.concatenate([jnp.where(m, a_t, 0.0), jnp.where(m, r_t, 0.0)], axis=0)
        g = _dot_nt(ar, bk)
        lows.append(jnp.where(strict, -g[:tt, :tt], 0.0))
        a_ak = jnp.where(strict, g[:tt, tt:], 0.0)
        p_rb.append(jnp.where(causal, g[tt:, :tt], 0.0))
        p_rk = jnp.where(causal, g[tt:, tt:], 0.0)
        rhs.append(jnp.concatenate([_dot(a_ak, v), a_t], axis=1))
        y0_parts.append(_dot(p_rk, v))
    return dict(lows=lows, rhs=rhs, y0=by_head(y0_parts), p_rb=p_rb, by_head=by_head,
                b_t=b_t, k_t=k_t, r_t=r_t, cum=cum, k2=k2, r=r, v=v)


def _head_sum(x, hd):
    lane = lax.broadcasted_iota(jnp.int32, (1, x.shape[1]), 1)
    tot = jnp.zeros_like(x)
    for j in range(x.shape[1] // hd):
        m = (lane >= j * hd) & (lane < (j + 1) * hd)
        tot = tot + jnp.where(m, jnp.sum(jnp.where(m, x, 0.0), axis=-1, keepdims=True), 0.0)
    return tot


def _rwkv_core_kernel(r_ref, k_ref, v_ref, w_ref, a_ref, g_ref, kk_ref, ka_ref, rk_ref, lnw_ref, lnb_ref,
                      y_ref, s_ref, *, hd, chunk):
    @pl.when(pl.program_id(2) == 0)
    def _init():
        s_ref[...] = jnp.zeros_like(s_ref)

    tt, width = r_ref.shape[1:]
    shift = chunk.bit_length() - 1
    rowi = lax.broadcasted_iota(jnp.int32, (tt, tt), 0)
    coli = lax.broadcasted_iota(jnp.int32, (tt, tt), 1)
    same_chunk = (rowi >> shift) == (coli >> shift)
    causal = same_chunk & (rowi >= coli)
    strict = same_chunk & (rowi > coli)
    srow = lax.broadcasted_iota(jnp.int32, (LANES, LANES), 0)
    scol = lax.broadcasted_iota(jnp.int32, (LANES, LANES), 1)
    same_head = None
    for j in range(LANES // hd):
        blk = (srow >= j * hd) & (srow < (j + 1) * hd) & (scol >= j * hd) & (scol < (j + 1) * hd)
        same_head = blk if same_head is None else (same_head | blk)

    groups = []
    for grp in range(width // LANES):
        ls = slice(grp * LANES, (grp + 1) * LANES)
        loc = _rwkv_chunk_local(r_ref[0, :, ls], k_ref[0, :, ls], v_ref[0, :, ls], w_ref[0, :, ls], a_ref[0, :, ls],
                                kk_ref[:, ls], ka_ref[:, ls], causal, strict, hd=hd, chunk=chunk)
        loc.update(state=s_ref[grp], us=[], rs=[])
        groups.append(loc)
    t_invs = _tri_inv([low for gp in groups for low in gp["lows"]], chunk)
    for gi, gp in enumerate(groups):
        n_h = len(gp["lows"])
        sols = [_dot(t_inv, rhs) for t_inv, rhs in zip(t_invs[gi * n_h:(gi + 1) * n_h], gp["rhs"])]
        gp.update(u0=gp["by_head"]([s[:, :LANES] for s in sols]), w_t=gp["by_head"]([s[:, LANES:] for s in sols]))

    for c in range(tt // chunk):
        sl = slice(c * chunk, (c + 1) * chunk)
        for gp in groups:
            u_c = gp["u0"][sl] + _dot_nt(gp["w_t"][sl], gp["state"])
            gp["rs"].append(_dot_nt(gp["r_t"][sl], gp["state"]))
            decay_end = jnp.exp(gp["cum"][(c + 1) * chunk - 1:(c + 1) * chunk, :])
            upd = gp["state"] + _dot_tn(u_c, gp["b_t"][sl]) + _dot_tn(gp["v"][sl], gp["k_t"][sl])
            gp["state"] = jnp.where(same_head, upd, 0.0) * decay_end
            gp["us"].append(u_c)

    for grp, gp in enumerate(groups):
        ls = slice(grp * LANES, (grp + 1) * LANES)
        s_ref[grp] = gp["state"]
        u = jnp.concatenate(gp["us"], axis=0)
        y = jnp.concatenate(gp["rs"], axis=0) + gp["y0"] + gp["by_head"]([_dot(p, u) for p in gp["p_rb"]])
        mean = _head_sum(y, hd) * (1.0 / hd)
        yc = y - mean
        var = _head_sum(yc * yc, hd) * (1.0 / hd)
        y = yc * lax.rsqrt(var + RWKV_GN_EPS) * lnw_ref[:, ls] + lnb_ref[:, ls]
        bonus = _head_sum(gp["r"] * gp["k2"] * rk_ref[:, ls], hd) * gp["v"]
        y_ref[0, :, ls] = ((y + bonus) * g_ref[0, :, ls]).astype(y_ref.dtype)


def _rwkv_mixer(x2d, g_in, shape, mu, w_rkv, w0, w1, w2, a0, a1, a2, g1, g2, k_k, k_a, r_k, ln_w, ln_b):
    bsz, t, d = shape
    hd = r_k.shape[1]
    x3 = x2d.reshape(bsz, t, d)
    tile = pl.BlockSpec((1, ROW_TILE, d), lambda b, i: (b, i, 0))
    vec = _resident((1, d))
    r, k, v, w, a, g = pl.pallas_call(
        _rwkv_in_kernel,
        grid=(bsz, t // ROW_TILE),
        in_specs=[tile, vec, _resident(mu.shape), _resident(w_rkv.shape), vec, _resident(w1.shape), _resident(w2.shape), vec, _resident(a1.shape), _resident(a2.shape),
                  _resident(g1.shape), _resident(g2.shape)],
        out_specs=[tile] * 6,
        out_shape=[jax.ShapeDtypeStruct((bsz, t, d), F32)] * 6,
        scratch_shapes=[pltpu.VMEM((SUBLANES, d), F32)],
        compiler_params=_params(2, 48),
        name="rwkv_in",
    )(x3, g_in.reshape(1, d), mu, w_rkv.astype(BF16), w0.reshape(1, d), w1.astype(BF16), w2.astype(BF16),
      a0.reshape(1, d), a1.astype(BF16), a2.astype(BF16), g1.astype(BF16), g2.astype(BF16))

    width = RWKV_WIDTH
    blk = pl.BlockSpec((1, TIME_TILE, width), lambda b, j, i: (b, i, j))
    pvec = pl.BlockSpec((1, width), lambda b, j, i: (0, j))
    y = pl.pallas_call(
        functools.partial(_rwkv_core_kernel, hd=hd, chunk=RWKV_CHUNK),
        grid=(bsz, d // width, t // TIME_TILE),
        in_specs=[blk] * 6 + [pvec] * 5,
        out_specs=blk,
        out_shape=jax.ShapeDtypeStruct((bsz, t, d), BF16),
        scratch_shapes=[pltpu.VMEM((width // LANES, LANES, LANES), F32)],
        compiler_params=_params(3, 48),
        name="rwkv_core",
    )(r, k, v, w, a, g, k_k.reshape(1, d), k_a.reshape(1, d), r_k.reshape(1, d), ln_w.reshape(1, d),
      ln_b.reshape(1, d))
    return y.reshape(bsz * t, d)


def _ssd_kernel(z_ref, x_ref, b_ref, c_ref, dtc_ref, dtr_ref, cwx_ref, cbx_ref, cwb_ref, cbb_ref, cwc_ref, cbc_ref,
                dtbc_ref, dtbr_ref, alc_ref, alr_ref, dskip_ref, nw_ref, y_ref, tx_ref, tb_ref, tc_ref, h_ref,
                *, heads, hp, chunk):
    @pl.when(pl.program_id(2) == 0)
    def _init():
        tx_ref[...] = jnp.zeros_like(tx_ref)
        tb_ref[...] = jnp.zeros_like(tb_ref)
        tc_ref[...] = jnp.zeros_like(tc_ref)
        h_ref[...] = jnp.zeros_like(h_ref)

    xs = _silu(_causal_conv(x_ref[0], tx_ref, cwx_ref[...]) + cbx_ref[...])
    bm = _silu(_causal_conv(b_ref[0], tb_ref, cwb_ref[...]) + cbb_ref[...])
    cm = _silu(_causal_conv(c_ref[0], tc_ref, cwc_ref[...]) + cbc_ref[...])
    tt, width = xs.shape
    dt_c = _softplus(dtc_ref[0, 0] + dtbc_ref[0])
    dt_r = _softplus(dtr_ref[0, 0] + dtbr_ref[0])
    da_c = dt_c * -jnp.exp(alc_ref[0])
    da_r = dt_r * -jnp.exp(alr_ref[0])
    acs_c = _seg_cumsum(da_c, 0, chunk)
    acs_r = _seg_cumsum(da_r, 1, chunk)

    lane = lax.broadcasted_iota(jnp.int32, (1, width), 1)
    head_of_lane = [(lane >= e * hp) & (lane < (e + 1) * hp) for e in range(heads)]

    def expand(cols):
        out = jnp.broadcast_to(cols[:, heads - 1:heads], (cols.shape[0], width))
        for e in range(heads - 1):
            out = jnp.where(head_of_lane[e], cols[:, e:e + 1], out)
        return out

    rowi = lax.broadcasted_iota(jnp.int32, (chunk, chunk), 0)
    coli = lax.broadcasted_iota(jnp.int32, (chunk, chunk), 1)
    causal = rowi >= coli
    pair_lo = lax.broadcasted_iota(jnp.int32, (1, 2 * hp), 1) < hp

    state = h_ref[...]
    ys = []
    for c in range(tt // chunk):
        sl = slice(c * chunk, (c + 1) * chunk)
        xc, bc, cc = xs[sl], bm[sl], cm[sl]
        ac_c = acs_c[sl]
        ac_r = acs_r[:, sl]
        tot = ac_c[chunk - 1:chunk, :]
        xdt = xc * expand(dt_c[sl])
        cb = _dot_nt(cc, bc)
        y_pairs = []
        for p in range(heads // 2):
            xp = xdt[:, 2 * p * hp:2 * (p + 1) * hp]
            halves = []
            for e in (2 * p, 2 * p + 1):
                seg = jnp.where(causal, ac_c[:, e:e + 1] - ac_r[e:e + 1, :], -jnp.inf)
                halves.append(_dot(cb * jnp.exp(seg), xp))
            y_pairs.append(jnp.where(pair_lo, halves[0], halves[1]))
        y_diag = jnp.concatenate(y_pairs, axis=1)
        y_off = _dot(cc, state) * expand(jnp.exp(ac_c))
        new = _dot_tn(bc, xdt * expand(jnp.exp(tot - ac_c)))
        state = state * expand(jnp.exp(tot)) + new
        ys.append(y_diag + y_off + xc * dskip_ref[0])
    h_ref[...] = state

    y = jnp.concatenate(ys, axis=0) * _silu(z_ref[0])
    y_ref[0] = _rms(y, nw_ref[0], SSD_NORM_EPS).astype(y_ref.dtype)


def _ssd_mixer(x2d, g_in, shape, w_in, conv_w, conv_b, dt_bias, a_log, d_skip, norm_w):
    bsz, t, d = shape
    n_heads = a_log.shape[0]
    inner = norm_w.shape[0]
    hp = inner // n_heads
    n_xbc = conv_w.shape[1]
    groups = SSD_GROUPS
    ns = (n_xbc - inner) // (2 * groups)
    gh = n_heads // groups
    gw = inner // groups
    o_b, o_c, o_dt = 2 * inner, 2 * inner + groups * ns, inner + n_xbc
    z, xs, bm, cm, dt = _norm_matmul(x2d, g_in, w_in.astype(BF16), [0, inner, o_b, o_c, o_dt, w_in.shape[1]], "ssd_in")
    dt = dt.reshape(bsz, t, groups, gh)
    dt_col = dt.transpose(0, 2, 1, 3)
    dt_row = dt.transpose(0, 2, 3, 1)

    tile = SSD_TIME_TILE

    def col(width):
        return pl.BlockSpec((1, tile, width), lambda b, j, i: (b, i, j))

    def wspec(rows, width):
        return pl.BlockSpec((rows, width), lambda b, j, i: (0, j))

    def grp(shape2):
        return pl.BlockSpec((1,) + shape2, lambda b, j, i: (j, 0, 0))

    gn = groups * ns
    cw_x, cw_b, cw_c = conv_w[:, :inner], conv_w[:, inner:inner + gn], conv_w[:, inner + gn:]
    cb = conv_b.reshape(1, n_xbc)
    cb_x, cb_b, cb_c = cb[:, :inner], cb[:, inner:inner + gn], cb[:, inner + gn:]
    y = pl.pallas_call(
        functools.partial(_ssd_kernel, heads=gh, hp=hp, chunk=SSD_CHUNK),
        grid=(bsz, groups, t // tile),
        in_specs=[col(gw), col(gw), col(ns), col(ns),
                  pl.BlockSpec((1, 1, tile, gh), lambda b, j, i: (b, j, i, 0)),
                  pl.BlockSpec((1, 1, gh, tile), lambda b, j, i: (b, j, 0, i)),
                  wspec(CONV_TAPS, gw), wspec(1, gw), wspec(CONV_TAPS, ns), wspec(1, ns),
                  wspec(CONV_TAPS, ns), wspec(1, ns),
                  grp((1, gh)), grp((gh, 1)), grp((1, gh)), grp((gh, 1)), grp((1, gw)), grp((1, gw))],
        out_specs=col(gw),
        out_shape=jax.ShapeDtypeStruct((bsz, t, inner), BF16),
        scratch_shapes=[pltpu.VMEM((SUBLANES, gw), F32), pltpu.VMEM((SUBLANES, ns), F32),
                        pltpu.VMEM((SUBLANES, ns), F32), pltpu.VMEM((ns, gw), F32)],
        compiler_params=_params(3, 32),
        name="ssd_core",
    )(z.reshape(bsz, t, inner), xs.reshape(bsz, t, inner), bm.reshape(bsz, t, -1), cm.reshape(bsz, t, -1),
      dt_col, dt_row, cw_x, cb_x, cw_b, cb_b, cw_c, cb_c,
      dt_bias.reshape(groups, 1, gh), dt_bias.reshape(groups, gh, 1),
      a_log.reshape(groups, 1, gh), a_log.reshape(groups, gh, 1),
      jnp.repeat(d_skip, hp).reshape(groups, 1, gw), norm_w.reshape(groups, 1, gw))
    return y.reshape(bsz * t, inner)


def _lru_kernel(gate_ref, xb_ref, cw_ref, cb_ref, wg_ref, bg_ref, lam_ref, y_ref, tail_ref, h_ref, a_ref, b_ref):
    @pl.when(pl.program_id(1) == 0)
    def _init():
        tail_ref[...] = jnp.zeros_like(tail_ref)
        h_ref[...] = jnp.zeros_like(h_ref)

    u = _causal_conv(xb_ref[0], tail_ref, cw_ref[...]) + cb_ref[...]
    tt = u.shape[0]
    blocks, bw, _ = wg_ref.shape
    groups = (tt // SUBLANES, SUBLANES, bw)
    pos = lax.broadcasted_iota(jnp.int32, groups, 1)
    for n in range(blocks):
        ls = slice(n * bw, (n + 1) * bw)
        un = u[:, ls]
        gates = _dot(un, wg_ref[n])
        r = _sigmoid(gates[:, :bw] + bg_ref[0:1, ls])
        i = _sigmoid(gates[:, bw:] + bg_ref[1:2, ls])
        log_a = -LRU_C * r * _softplus(-lam_ref[:, ls])
        a = jnp.exp(log_a)
        b = jnp.sqrt(1.0 - jnp.exp(2.0 * log_a)) * (i * un)
        a, b = a.reshape(groups), b.reshape(groups)
        s = 1
        while s < SUBLANES:
            keep = pos >= s
            b = jnp.where(keep, a * pltpu.roll(b, s, 1) + b, b)
            a = jnp.where(keep, a * pltpu.roll(a, s, 1), a)
            s *= 2
        a_ref[:, ls] = a.reshape(tt, bw)
        b_ref[:, ls] = b.reshape(tt, bw)

    def carry(g, h):
        rows = pl.ds(pl.multiple_of(g * SUBLANES, SUBLANES), SUBLANES)
        hg = b_ref[rows, :] + a_ref[rows, :] * h
        b_ref[rows, :] = hg
        return hg[SUBLANES - 1:SUBLANES, :]

    h_ref[...] = lax.fori_loop(0, tt // SUBLANES, carry, h_ref[...], unroll=8)
    y_ref[0] = (_gelu_tanh(gate_ref[0]) * b_ref[...]).astype(y_ref.dtype)


def _lru_mixer(x2d, g_in, shape, w_in, conv_w, conv_b, w_gates, b_gates, lam):
    bsz, t, d = shape
    width = lam.shape[0]
    gate, xb = _norm_matmul(x2d, g_in, w_in.astype(BF16), [0, width, 2 * width], "lru_in")
    blk = pl.BlockSpec((1, TIME_TILE, width), lambda b, i: (b, i, 0))
    y = pl.pallas_call(
        _lru_kernel,
        grid=(bsz, t // TIME_TILE),
        in_specs=[blk, blk, _resident(conv_w.shape), _resident((1, width)), _resident(w_gates.shape),
                  _resident(b_gates.shape), _resident((1, width))],
        out_specs=blk,
        out_shape=jax.ShapeDtypeStruct((bsz, t, width), BF16),
        scratch_shapes=[pltpu.VMEM((SUBLANES, width), F32), pltpu.VMEM((1, width), F32),
                        pltpu.VMEM((TIME_TILE, width), F32), pltpu.VMEM((TIME_TILE, width), F32)],
        compiler_params=_params(2, 32),
        name="lru_core",
    )(gate.reshape(bsz, t, width), xb.reshape(bsz, t, width), conv_w, conv_b.reshape(1, width),
      w_gates.astype(BF16), b_gates, lam.reshape(1, width))
    return y.reshape(bsz * t, width)


def kernel(x, norm_g, ffn_w_in, ffn_w_out, gdn_w_in, gdn_conv_w, gdn_a_log, gdn_dt_bias, gdn_norm_w, gdn_w_out, rwkv_mu, rwkv_w_rkv, rwkv_w0, rwkv_w1, rwkv_w2, rwkv_a0, rwkv_a1, rwkv_a2, rwkv_g1, rwkv_g2, rwkv_k_k, rwkv_k_a, rwkv_r_k, rwkv_ln_w, rwkv_ln_b, rwkv_w_out, ssd_w_in, ssd_conv_w, ssd_conv_b, ssd_dt_bias, ssd_a_log, ssd_d, ssd_norm_w, ssd_w_out, lru_w_in, lru_conv_w, lru_conv_b, lru_w_gates, lru_b_gates, lru_lam, lru_w_out):
    shape = x.shape
    bsz, t, d = shape
    depth = norm_g.shape[0]
    n_mixers = 4
    x2d = x.reshape(bsz * t, d)
    for i in range(depth):
        m, j = i % n_mixers, i // n_mixers
        g = norm_g[i]
        x2d = _ffn(x2d, g[0], g[1], ffn_w_in[i, 0].astype(BF16), ffn_w_out[i, 0].astype(BF16))
        if m == 0:
            y = _gdn_mixer(x2d, g[2], shape, gdn_w_in[j], gdn_conv_w[j], gdn_a_log[j], gdn_dt_bias[j], gdn_norm_w[j])
            w_o = gdn_w_out[j]
        elif m == 1:
            y = _rwkv_mixer(x2d, g[2], shape, rwkv_mu[j], rwkv_w_rkv[j], rwkv_w0[j], rwkv_w1[j], rwkv_w2[j],
                            rwkv_a0[j], rwkv_a1[j], rwkv_a2[j], rwkv_g1[j], rwkv_g2[j], rwkv_k_k[j], rwkv_k_a[j],
                            rwkv_r_k[j], rwkv_ln_w[j], rwkv_ln_b[j])
            w_o = rwkv_w_out[j]
        elif m == 2:
            y = _ssd_mixer(x2d, g[2], shape, ssd_w_in[j], ssd_conv_w[j], ssd_conv_b[j], ssd_dt_bias[j],
                           ssd_a_log[j], ssd_d[j], ssd_norm_w[j])
            w_o = ssd_w_out[j]
        else:
            y = _lru_mixer(x2d, g[2], shape, lru_w_in[j], lru_conv_w[j], lru_conv_b[j], lru_w_gates[j],
                           lru_b_gates[j], lru_lam[j])
            w_o = lru_w_out[j]
        x2d = _mix_out_ffn(y, w_o.astype(BF16), g[3], x2d, g[4], g[5], ffn_w_in[i, 1].astype(BF16),
                           ffn_w_out[i, 1].astype(BF16))
    return x2d.reshape(shape)
```

```python
import functools

import jax
import jax.numpy as jnp
from jax import lax
from jax.experimental import pallas as pl
from jax.experimental.pallas import tpu as pltpu

F32 = jnp.float32
BF16 = jnp.bfloat16

V7X_VMEM_BYTES = 64 * 1024 * 1024
SUBLANES = 8
LANES = 128
MXU_WIDTH = 256

NORM_EPS = 1e-6
L2_EPS = 1e-6
SSD_NORM_EPS = 1e-5
RWKV_GN_EPS = 64e-5
LRU_C = 8.0

ROW_TILE = 512
TIME_TILE = 256
GDN_CHUNK = 64
RWKV_CHUNK = 64
SSD_CHUNK = 128
SSD_TIME_TILE = 512
SSD_GROUPS = 4
CONV_TAPS = 4


def _params(n_axes, vmem_mib):
    return pltpu.CompilerParams(
        dimension_semantics=("arbitrary",) * n_axes,
        vmem_limit_bytes=vmem_mib * 1024 * 1024,
    )


def _resident(shape):
    zeros = (0,) * len(shape)
    return pl.BlockSpec(shape, lambda *_: zeros, pipeline_mode=pl.Buffered(1))


def _slab(arr, idx):
    tail = tuple(arr.shape[len(idx):])
    at = tuple(idx) + (0,) * len(tail)
    return pl.BlockSpec((None,) * len(idx) + tail, lambda *_: at, pipeline_mode=pl.Buffered(1))


def _rms(x, g, eps):
    return x * lax.rsqrt(jnp.mean(x * x, axis=-1, keepdims=True) + eps) * g


def _sigmoid(x):
    return 1.0 / (1.0 + jnp.exp(-x))


def _silu(x):
    return x * _sigmoid(x)


def _softplus(x):
    return jnp.maximum(x, 0.0) + jnp.log1p(jnp.exp(-jnp.abs(x)))


def _gelu_tanh(x):
    return 0.5 * x * (1.0 + jnp.tanh(0.7978845608028654 * (x + 0.044715 * (x * x * x))))


def _dot(a, b):
    return jnp.dot(a.astype(BF16), b.astype(BF16), preferred_element_type=F32)


def _dot_nt(a, b):
    return lax.dot_general(a.astype(BF16), b.astype(BF16), (((1,), (1,)), ((), ())),
                           preferred_element_type=F32)


def _dot_tn(a, b):
    return lax.dot_general(a.astype(BF16), b.astype(BF16), (((0,), (0,)), ((), ())),
                           preferred_element_type=F32)


def _seg_cumsum(x, axis, seg):
    pos = lax.broadcasted_iota(jnp.int32, x.shape, axis) & (seg - 1)
    s = 1
    while s < seg:
        x = x + jnp.where(pos >= s, pltpu.roll(x, s, axis), 0.0)
        s *= 2
    return x


def _causal_conv(pre, tail_ref, w):
    tt = pre.shape[0]
    ext = jnp.concatenate([tail_ref[...], pre], axis=0)
    y = pre * w[CONV_TAPS - 1:CONV_TAPS, :]
    for k in range(1, CONV_TAPS):
        y = y + pltpu.roll(ext, k, 0)[SUBLANES:, :] * w[CONV_TAPS - 1 - k:CONV_TAPS - k, :]
    tail_ref[...] = pre[tt - SUBLANES:, :]
    return y


def _tri_inv(lows, block):
    n = lows[0].shape[0]
    n_sys = n // block
    row = lax.broadcasted_iota(jnp.int32, (n, n), 0)
    col = lax.broadcasted_iota(jnp.int32, (n, n), 1)
    on_diag = (row >> (block.bit_length() - 1)) == (col >> (block.bit_length() - 1))

    def to_diag(m_l):
        return jnp.where(on_diag, jnp.concatenate([m_l] * n_sys, axis=0), 0.0)

    row_l = lax.broadcasted_iota(jnp.int32, (block, n), 0)
    col_l = lax.broadcasted_iota(jnp.int32, (block, n), 1) & (block - 1)
    eye_l = jnp.where(row_l == col_l, 1.0, 0.0)
    pair_l = (row_l >> 1) == (col_l >> 1)
    t_ls = []
    for low in lows:
        low_l = low[:block]
        for s in range(1, n_sys):
            low_l = low_l + low[s * block:(s + 1) * block]
        t_ls.append(eye_l - jnp.where(pair_l, low_l, 0.0))
    b, sh = 2, 1
    while b < block:
        sub = ((row >> (sh + 1)) == (col >> (sh + 1))) & (((row >> sh) & 1) == 1) & (((col >> sh) & 1) == 0)
        xs = [_dot(t_l, jnp.where(sub, low, 0.0)) for t_l, low in zip(t_ls, lows)]
        t_ls = [t_l - _dot(x, to_diag(t_l)) for t_l, x in zip(t_ls, xs)]
        b, sh = 2 * b, sh + 1
    return [to_diag(t_l) for t_l in t_ls]


def _ffn_half_step(x, g0, g1, win_ref, wout_ref):
    d_ff = wout_ref.shape[0]
    mid = (d_ff // MXU_WIDTH // 2) * MXU_WIDTH
    xn = _rms(x, g0, NORM_EPS).astype(BF16)
    acc = None
    for lo, hi in ((0, mid), (mid, d_ff)) if 0 < mid < d_ff else ((0, d_ff),):
        gate = jnp.dot(xn, win_ref[:, lo:hi], preferred_element_type=F32)
        up = jnp.dot(xn, win_ref[:, d_ff + lo:d_ff + hi], preferred_element_type=F32)
        act = (_silu(gate) * up).astype(BF16)
        part = jnp.dot(act, wout_ref[lo:hi, :], preferred_element_type=F32)
        acc = part if acc is None else acc + part
    return x + 0.5 * _rms(acc, g1, NORM_EPS)


def _ffn_kernel(x_ref, g0_ref, g1_ref, win_ref, wout_ref, o_ref):
    o_ref[...] = _ffn_half_step(x_ref[...], g0_ref[...], g1_ref[...], win_ref, wout_ref)


def _ffn(x2d, g0, g1, w_in, w_out, idx):
    n, d = x2d.shape
    row = pl.BlockSpec((ROW_TILE, d), lambda i: (i, 0))
    return pl.pallas_call(
        _ffn_kernel,
        grid=(n // ROW_TILE,),
        in_specs=[row, _resident((1, d)), _resident((1, d)), _slab(w_in, idx), _slab(w_out, idx)],
        out_specs=row,
        out_shape=jax.ShapeDtypeStruct((n, d), F32),
        compiler_params=_params(1, 56),
        name="ffn",
    )(x2d, g0.reshape(1, d), g1.reshape(1, d), w_in, w_out)


def _mix_out_ffn_kernel(y_ref, wo_ref, go_ref, x_ref, g0_ref, g1_ref, win_ref, wout_ref, o_ref):
    h = jnp.dot(y_ref[...], wo_ref[...], preferred_element_type=F32)
    x = x_ref[...] + _rms(h, go_ref[...], NORM_EPS)
    o_ref[...] = _ffn_half_step(x, g0_ref[...], g1_ref[...], win_ref, wout_ref)


def _mix_out_ffn(y2d, w_o, g_o, x2d, g0, g1, w_in, w_out, idx):
    n, d = x2d.shape
    k = y2d.shape[1]
    row = pl.BlockSpec((ROW_TILE, d), lambda i: (i, 0))
    vec = _resident((1, d))
    return pl.pallas_call(
        _mix_out_ffn_kernel,
        grid=(n // ROW_TILE,),
        in_specs=[pl.BlockSpec((ROW_TILE, k), lambda i: (i, 0)), _resident(w_o.shape), vec, row, vec, vec,
                  _slab(w_in, idx), _slab(w_out, idx)],
        out_specs=row,
        out_shape=jax.ShapeDtypeStruct((n, d), F32),
        compiler_params=_params(1, 56),
        name="mix_out_ffn",
    )(y2d, w_o, g_o.reshape(1, d), x2d, g0.reshape(1, d), g1.reshape(1, d), w_in, w_out)


def _gdn_chunk_local(q, k, v, beta, gc_col, gc_row, causal, strict):
    decay = jnp.exp(jnp.where(causal, gc_col - gc_row, -jnp.inf))
    lower = jnp.where(strict, _dot_nt(k, k) * decay * beta, 0.0)
    eg = jnp.exp(gc_col)
    rhs = jnp.concatenate([v * beta, k * (beta * eg)], axis=1)
    return lower, rhs, _dot_nt(q, k) * decay, q * eg


def _gdn_kernel(x_ref, gn_ref, w_ref, wbat_ref, cw_ref, alog_ref, dtb_ref, nw_ref, y_ref, tail_ref, s_ref,
                *, heads, dk, dv, chunk):
    @pl.when(pl.program_id(1) == 0)
    def _init():
        tail_ref[...] = jnp.zeros_like(tail_ref)
        s_ref[...] = jnp.zeros_like(s_ref)

    n_qkv = cw_ref.shape[1]
    o_k, o_v, o_ba = heads * dk, 2 * heads * dk, n_qkv + heads * dv
    xn = _rms(x_ref[0], gn_ref[...], NORM_EPS).astype(BF16)
    qkv = _silu(_causal_conv(jnp.dot(xn, w_ref[:, :n_qkv], preferred_element_type=F32), tail_ref, cw_ref[...]))
    z = jnp.dot(xn, w_ref[:, n_qkv:o_ba], preferred_element_type=F32)
    ba_c = jnp.dot(xn, w_ref[:, o_ba:], preferred_element_type=F32)
    ba_r = _dot_nt(wbat_ref[...], xn)
    tt = qkv.shape[0]
    shift = chunk.bit_length() - 1
    row = lax.broadcasted_iota(jnp.int32, (tt, tt), 0)
    col = lax.broadcasted_iota(jnp.int32, (tt, tt), 1)
    same_chunk = (row >> shift) == (col >> shift)
    causal = same_chunk & (row >= col)
    strict = same_chunk & (row > col)
    hds = []
    for h in range(heads):
        qh = qkv[:, h * dk:(h + 1) * dk]
        kh = qkv[:, o_k + h * dk:o_k + (h + 1) * dk]
        vh = qkv[:, o_v + h * dv:o_v + (h + 1) * dv]
        qh = qh * (lax.rsqrt(jnp.sum(qh * qh, axis=-1, keepdims=True) + L2_EPS) * dk ** -0.5)
        kh = kh * lax.rsqrt(jnp.sum(kh * kh, axis=-1, keepdims=True) + L2_EPS)
        beta = _sigmoid(ba_c[:, h:h + 1])
        neg_a = -jnp.exp(alog_ref[h])
        gc_col = _seg_cumsum(neg_a * _softplus(ba_c[:, heads + h:heads + h + 1] + dtb_ref[h]), 0, chunk)
        gc_row = _seg_cumsum(neg_a * _softplus(ba_r[heads + h:heads + h + 1, :] + dtb_ref[h]), 1, chunk)
        lower, rhs, qk, q_dec = _gdn_chunk_local(qh, kh, vh, beta, gc_col, gc_row, causal, strict)
        hds.append(dict(lower=lower, rhs=rhs, qk=qk, q_dec=q_dec, k=kh, gc=gc_col, state=s_ref[h],
                        v_new=[], o_state=[]))
    for hd, t_inv in zip(hds, _tri_inv([hd["lower"] for hd in hds], chunk)):
        sol = _dot(t_inv, hd["rhs"])
        hd.update(u=sol[:, :dv], w=sol[:, dv:])

    for c in range(tt // chunk):
        sl = slice(c * chunk, (c + 1) * chunk)
        for hd in hds:
            g_last = hd["gc"][(c + 1) * chunk - 1:(c + 1) * chunk, :]
            vn = hd["u"][sl] - _dot(hd["w"][sl], hd["state"])
            hd["o_state"].append(_dot(hd["q_dec"][sl], hd["state"]))
            hd["state"] = hd["state"] * jnp.exp(g_last) + _dot_tn(hd["k"][sl] * jnp.exp(g_last - hd["gc"][sl]), vn)
            hd["v_new"].append(vn)

    for h, hd in enumerate(hds):
        s_ref[h] = hd["state"]
        o = jnp.concatenate(hd["o_state"], axis=0) + _dot(hd["qk"], jnp.concatenate(hd["v_new"], axis=0))
        o = _rms(o, nw_ref[...], NORM_EPS) * _silu(z[:, h * dv:(h + 1) * dv])
        y_ref[0, :, h * dv:(h + 1) * dv] = o.astype(y_ref.dtype)


def _gdn_mixer(x2d, g_in, shape, w_in, conv_w, a_log, dt_bias, norm_w):
    bsz, t, d = shape
    heads = a_log.shape[0]
    dv = norm_w.shape[0]
    n_qkv = conv_w.shape[1]
    dk = (n_qkv - heads * dv) // (2 * heads)
    w = w_in.astype(BF16)
    w_ba_t = w[:, n_qkv + heads * dv:].T
    tile = pl.BlockSpec((1, TIME_TILE, d), lambda b, i: (b, i, 0))
    y = pl.pallas_call(
        functools.partial(_gdn_kernel, heads=heads, dk=dk, dv=dv, chunk=GDN_CHUNK),
        grid=(bsz, t // TIME_TILE),
        in_specs=[tile, _resident((1, d)), _resident(w.shape), _resident(w_ba_t.shape), _resident(conv_w.shape),
                  _resident((heads, 1, 1)), _resident((heads, 1, 1)), _resident((1, dv))],
        out_specs=pl.BlockSpec((1, TIME_TILE, heads * dv), lambda b, i: (b, i, 0)),
        out_shape=jax.ShapeDtypeStruct((bsz, t, heads * dv), BF16),
        scratch_shapes=[pltpu.VMEM((SUBLANES, n_qkv), F32), pltpu.VMEM((heads, dk, dv), F32)],
        compiler_params=_params(2, 56),
        name="gdn",
    )(x2d.reshape(bsz, t, d), g_in.reshape(1, d), w, w_ba_t, conv_w, a_log.reshape(heads, 1, 1),
      dt_bias.reshape(heads, 1, 1), norm_w.reshape(1, dv))
    return y.reshape(bsz * t, heads * dv)


def _rwkv_project(x_ref, g_ref, mu_ref, wrkv_ref, w0_ref, w1_ref, w2_ref, a0_ref, a1_ref, a2_ref,
                  g1_ref, g2_ref, r_ref, k_ref, v_ref, w_ref, a_ref, go_ref, prev_ref):
    hn = _rms(x_ref[0], g_ref[...], NORM_EPS)
    tm = hn.shape[0]
    ext = jnp.concatenate([prev_ref[...], hn], axis=0)
    xx = pltpu.roll(ext, 1, 0)[SUBLANES:, :] - hn
    prev_ref[...] = hn[tm - SUBLANES:, :]
    mu = mu_ref[...]

    def mix(i):
        return (hn + xx * mu[i:i + 1, :]).astype(BF16)

    r_ref[0] = jnp.dot(mix(0), wrkv_ref[0], preferred_element_type=F32)
    k_ref[0] = jnp.dot(mix(1), wrkv_ref[1], preferred_element_type=F32)
    v_ref[0] = jnp.dot(mix(2), wrkv_ref[2], preferred_element_type=F32)
    w_ref[0] = w0_ref[...] + _dot(jnp.tanh(jnp.dot(mix(3), w1_ref[...], preferred_element_type=F32)), w2_ref[...])
    a_ref[0] = a0_ref[...] + _dot(jnp.dot(mix(4), a1_ref[...], preferred_element_type=F32), a2_ref[...])
    go_ref[0] = _dot(_sigmoid(jnp.dot(mix(5), g1_ref[...], preferred_element_type=F32)), g2_ref[...])


def _rwkv_chunk_local(r, k, v, w_pre, a_pre, kk_w, ka_w, causal, strict, *, hd, chunk):
    tt, width = r.shape
    n_heads = width // hd
    lane = lax.broadcasted_iota(jnp.int32, (1, width), 1)
    head_of_lane = [(lane >= j * hd) & (lane < (j + 1) * hd) for j in range(n_heads)]

    def by_head(parts):
        out = parts[-1]
        for m, p in zip(head_of_lane[:-1], parts[:-1]):
            out = jnp.where(m, p, out)
        return out

    log_w = -jnp.exp(-_softplus(-w_pre) - 0.5)
    a = _sigmoid(a_pre)
    kk = k * kk_w
    kk = kk * lax.rsqrt(_head_sum(kk * kk, hd) + L2_EPS)
    k2 = k * (1.0 + (a - 1.0) * ka_w)
    cum = _seg_cumsum(log_w, 0, chunk)
    inv = jnp.exp(-cum)
    a_t = -kk * jnp.exp(cum - log_w)
    b_t = kk * a * inv
    k_t = k2 * inv
    r_t = r * jnp.exp(cum)

    bk = jnp.concatenate([b_t, k_t], axis=0)
    lows, rhs, y0_parts, p_rb = [], [], [], []
    for m in head_of_lane:
        ar = jnp.concatenate([jnp.where(m, a_t, 0.0), jnp.where(m, r_t, 0.0)], axis=0)
        g = _dot_nt(ar, bk)
        lows.append(jnp.where(strict, -g[:tt, :tt], 0.0))
        a_ak = jnp.where(strict, g[:tt, tt:], 0.0)
        p_rb.append(jnp.where(causal, g[tt:, :tt], 0.0))
        p_rk = jnp.where(causal, g[tt:, tt:], 0.0)
        rhs.append(jnp.concatenate([_dot(a_ak, v), a_t], axis=1))
        y0_parts.append(_dot(p_rk, v))
    return dict(lows=lows, rhs=rhs, y0=by_head(y0_parts), p_rb=p_rb, by_head=by_head,
                b_t=b_t, k_t=k_t, r_t=r_t, cum=cum, k2=k2, r=r, v=v)


def _head_sum(x, hd):
    lane = lax.broadcasted_iota(jnp.int32, (1, x.shape[1]), 1)
    tot = jnp.zeros_like(x)
    for j in range(x.shape[1] // hd):
        m = (lane >= j * hd) & (lane < (j + 1) * hd)
        tot = tot + jnp.where(m, jnp.sum(jnp.where(m, x, 0.0), axis=-1, keepdims=True), 0.0)
    return tot


def _rwkv_kernel(x_ref, gn_ref, mu_ref, wrkv_ref, w0_ref, w1_ref, w2_ref, a0_ref, a1_ref, a2_ref, g1_ref, g2_ref,
                 kk_ref, ka_ref, rk_ref, lnw_ref, lnb_ref, y_ref,
                 prev_ref, s_ref, r_ref, k_ref, v_ref, w_ref, a_ref, g_ref, *, hd, chunk):
    @pl.when(pl.program_id(1) == 0)
    def _init():
        prev_ref[...] = jnp.zeros_like(prev_ref)
        s_ref[...] = jnp.zeros_like(s_ref)

    _rwkv_project(x_ref, gn_ref, mu_ref, wrkv_ref, w0_ref, w1_ref, w2_ref, a0_ref, a1_ref, a2_ref, g1_ref, g2_ref,
                  r_ref, k_ref, v_ref, w_ref, a_ref, g_ref, prev_ref)
    _rwkv_recurrence(r_ref, k_ref, v_ref, w_ref, a_ref, g_ref, kk_ref, ka_ref, rk_ref, lnw_ref, lnb_ref,
                     y_ref, s_ref, hd=hd, chunk=chunk)


def _rwkv_recurrence(r_ref, k_ref, v_ref, w_ref, a_ref, g_ref, kk_ref, ka_ref, rk_ref, lnw_ref, lnb_ref,
                     y_ref, s_ref, *, hd, chunk):
    tt, width = r_ref.shape[1:]
    shift = chunk.bit_length() - 1
    rowi = lax.broadcasted_iota(jnp.int32, (tt, tt), 0)
    coli = lax.broadcasted_iota(jnp.int32, (tt, tt), 1)
    same_chunk = (rowi >> shift) == (coli >> shift)
    causal = same_chunk & (rowi >= coli)
    strict = same_chunk & (rowi > coli)
    srow = lax.broadcasted_iota(jnp.int32, (LANES, LANES), 0)
    scol = lax.broadcasted_iota(jnp.int32, (LANES, LANES), 1)
    same_head = None
    for j in range(LANES // hd):
        blk = (srow >= j * hd) & (srow < (j + 1) * hd) & (scol >= j * hd) & (scol < (j + 1) * hd)
        same_head = blk if same_head is None else (same_head | blk)

    groups = []
    for grp in range(width // LANES):
        ls = slice(grp * LANES, (grp + 1) * LANES)
        loc = _rwkv_chunk_local(r_ref[0, :, ls], k_ref[0, :, ls], v_ref[0, :, ls], w_ref[0, :, ls], a_ref[0, :, ls],
                                kk_ref[:, ls], ka_ref[:, ls], causal, strict, hd=hd, chunk=chunk)
        loc.update(state=s_ref[grp], us=[], rs=[])
        groups.append(loc)
    t_invs = _tri_inv([low for gp in groups for low in gp["lows"]], chunk)
    for gi, gp in enumerate(groups):
        n_h = len(gp["lows"])
        sols = [_dot(t_inv, rhs) for t_inv, rhs in zip(t_invs[gi * n_h:(gi + 1) * n_h], gp["rhs"])]
        gp.update(u0=gp["by_head"]([s[:, :LANES] for s in sols]), w_t=gp["by_head"]([s[:, LANES:] for s in sols]))

    for c in range(tt // chunk):
        sl = slice(c * chunk, (c + 1) * chunk)
        for gp in groups:
            u_c = gp["u0"][sl] + _dot_nt(gp["w_t"][sl], gp["state"])
            gp["rs"].append(_dot_nt(gp["r_t"][sl], gp["state"]))
            decay_end = jnp.exp(gp["cum"][(c + 1) * chunk - 1:(c + 1) * chunk, :])
            upd = gp["state"] + _dot_tn(u_c, gp["b_t"][sl]) + _dot_tn(gp["v"][sl], gp["k_t"][sl])
            gp["state"] = jnp.where(same_head, upd, 0.0) * decay_end
            gp["us"].append(u_c)

    for grp, gp in enumerate(groups):
        ls = slice(grp * LANES, (grp + 1) * LANES)
        s_ref[grp] = gp["state"]
        u = jnp.concatenate(gp["us"], axis=0)
        y = jnp.concatenate(gp["rs"], axis=0) + gp["y0"] + gp["by_head"]([_dot(p, u) for p in gp["p_rb"]])
        mean = _head_sum(y, hd) * (1.0 / hd)
        yc = y - mean
        var = _head_sum(yc * yc, hd) * (1.0 / hd)
        y = yc * lax.rsqrt(var + RWKV_GN_EPS) * lnw_ref[:, ls] + lnb_ref[:, ls]
        bonus = _head_sum(gp["r"] * gp["k2"] * rk_ref[:, ls], hd) * gp["v"]
        y_ref[0, :, ls] = ((y + bonus) * g_ref[0, :, ls]).astype(y_ref.dtype)


def _rwkv_mixer(x2d, g_in, shape, mu, w_rkv, w0, w1, w2, a0, a1, a2, g1, g2, k_k, k_a, r_k, ln_w, ln_b):
    bsz, t, d = shape
    hd = r_k.shape[1]
    tile = pl.BlockSpec((1, TIME_TILE, d), lambda b, i: (b, i, 0))
    vec = _resident((1, d))
    proj = pltpu.VMEM((1, TIME_TILE, d), F32)
    y = pl.pallas_call(
        functools.partial(_rwkv_kernel, hd=hd, chunk=RWKV_CHUNK),
        grid=(bsz, t // TIME_TILE),
        in_specs=[tile, vec, _resident(mu.shape), _resident(w_rkv.shape), vec, _resident(w1.shape),
                  _resident(w2.shape), vec, _resident(a1.shape), _resident(a2.shape), _resident(g1.shape),
                  _resident(g2.shape)] + [vec] * 5,
        out_specs=tile,
        out_shape=jax.ShapeDtypeStruct((bsz, t, d), BF16),
        scratch_shapes=[pltpu.VMEM((SUBLANES, d), F32), pltpu.VMEM((d // LANES, LANES, LANES), F32)] + [proj] * 6,
        compiler_params=_params(2, 56),
        name="rwkv",
    )(x2d.reshape(bsz, t, d), g_in.reshape(1, d), mu, w_rkv.astype(BF16), w0.reshape(1, d), w1.astype(BF16),
      w2.astype(BF16), a0.reshape(1, d), a1.astype(BF16), a2.astype(BF16), g1.astype(BF16), g2.astype(BF16),
      k_k.reshape(1, d), k_a.reshape(1, d), r_k.reshape(1, d), ln_w.reshape(1, d), ln_b.reshape(1, d))
    return y.reshape(bsz * t, d)


def _ssd_kernel(x_ref, gn_ref, w_ref, wdtt_ref, cw_ref, cb_ref, dtbc_ref, dtbr_ref, alc_ref, alr_ref, dskip_ref, nw_ref,
                y_ref, tail_ref, h_ref, *, groups, gh, hp, ns, chunk):
    @pl.when(pl.program_id(1) == 0)
    def _init():
        tail_ref[...] = jnp.zeros_like(tail_ref)
        h_ref[...] = jnp.zeros_like(h_ref)

    gw = gh * hp
    inner = groups * gw
    n_xbc = cw_ref.shape[1]
    xn = _rms(x_ref[0], gn_ref[...], NORM_EPS).astype(BF16)
    z = jnp.dot(xn, w_ref[:, :inner], preferred_element_type=F32)
    xbc = jnp.dot(xn, w_ref[:, inner:inner + n_xbc], preferred_element_type=F32)
    xbc = _silu(_causal_conv(xbc, tail_ref, cw_ref[...]) + cb_ref[...])
    dt_c = _softplus(jnp.dot(xn, w_ref[:, inner + n_xbc:], preferred_element_type=F32) + dtbc_ref[...])
    dt_r = _softplus(_dot_nt(wdtt_ref[...], xn) + dtbr_ref[...])
    acs_c = _seg_cumsum(dt_c * -jnp.exp(alc_ref[...]), 0, chunk)
    acs_r = _seg_cumsum(dt_r * -jnp.exp(alr_ref[...]), 1, chunk)
    tt = xbc.shape[0]

    lane = lax.broadcasted_iota(jnp.int32, (1, gw), 1)
    head_of_lane = [(lane >= e * hp) & (lane < (e + 1) * hp) for e in range(gh)]

    def expand(cols, first):
        out = jnp.broadcast_to(cols[:, first + gh - 1:first + gh], (cols.shape[0], gw))
        for e in range(gh - 1):
            out = jnp.where(head_of_lane[e], cols[:, first + e:first + e + 1], out)
        return out

    rowi = lax.broadcasted_iota(jnp.int32, (chunk, chunk), 0)
    coli = lax.broadcasted_iota(jnp.int32, (chunk, chunk), 1)
    causal = rowi >= coli
    pair_lo = lax.broadcasted_iota(jnp.int32, (1, 2 * hp), 1) < hp

    states = [h_ref[g] for g in range(groups)]
    ys = [[] for _ in range(groups)]
    for c in range(tt // chunk):
        sl = slice(c * chunk, (c + 1) * chunk)
        for g in range(groups):
            first = g * gh
            xc = xbc[sl, g * gw:(g + 1) * gw]
            bc = xbc[sl, inner + g * ns:inner + (g + 1) * ns]
            cc = xbc[sl, inner + groups * ns + g * ns:inner + groups * ns + (g + 1) * ns]
            ac_c = acs_c[sl]
            ac_r = acs_r[:, sl]
            tot = ac_c[chunk - 1:chunk, :]
            xdt = xc * expand(dt_c[sl], first)
            cb = _dot_nt(cc, bc)
            y_pairs = []
            for p in range(gh // 2):
                xp = xdt[:, 2 * p * hp:2 * (p + 1) * hp]
                halves = []
                for e in (first + 2 * p, first + 2 * p + 1):
                    seg = jnp.where(causal, ac_c[:, e:e + 1] - ac_r[e:e + 1, :], -jnp.inf)
                    halves.append(_dot(cb * jnp.exp(seg), xp))
                y_pairs.append(jnp.where(pair_lo, halves[0], halves[1]))
            y_diag = jnp.concatenate(y_pairs, axis=1)
            y_off = _dot(cc, states[g]) * expand(jnp.exp(ac_c), first)
            new = _dot_tn(bc, xdt * expand(jnp.exp(tot - ac_c), first))
            states[g] = states[g] * expand(jnp.exp(tot), first) + new
            ys[g].append(y_diag + y_off + xc * dskip_ref[:, g * gw:(g + 1) * gw])
    for g in range(groups):
        h_ref[g] = states[g]
        ls = slice(g * gw, (g + 1) * gw)
        y = jnp.concatenate(ys[g], axis=0) * _silu(z[:, ls])
        y_ref[0, :, ls] = _rms(y, nw_ref[:, ls], SSD_NORM_EPS).astype(y_ref.dtype)


def _ssd_mixer(x2d, g_in, shape, w_in, conv_w, conv_b, dt_bias, a_log, d_skip, norm_w):
    bsz, t, d = shape
    n_heads = a_log.shape[0]
    inner = norm_w.shape[0]
    hp = inner // n_heads
    n_xbc = conv_w.shape[1]
    groups = SSD_GROUPS
    ns = (n_xbc - inner) // (2 * groups)
    gh = n_heads // groups
    w = w_in.astype(BF16)
    w_dt_t = w[:, inner + n_xbc:].T
    tile = SSD_TIME_TILE
    y = pl.pallas_call(
        functools.partial(_ssd_kernel, groups=groups, gh=gh, hp=hp, ns=ns, chunk=SSD_CHUNK),
        grid=(bsz, t // tile),
        in_specs=[pl.BlockSpec((1, tile, d), lambda b, i: (b, i, 0)), _resident((1, d)), _resident(w.shape),
                  _resident(w_dt_t.shape), _resident(conv_w.shape), _resident((1, n_xbc)),
                  _resident((1, n_heads)), _resident((n_heads, 1)), _resident((1, n_heads)), _resident((n_heads, 1)),
                  _resident((1, inner)), _resident((1, inner))],
        out_specs=pl.BlockSpec((1, tile, inner), lambda b, i: (b, i, 0)),
        out_shape=jax.ShapeDtypeStruct((bsz, t, inner), BF16),
        scratch_shapes=[pltpu.VMEM((SUBLANES, n_xbc), F32), pltpu.VMEM((groups, ns, inner // groups), F32)],
        compiler_params=_params(2, 56),
        name="ssd",
    )(x2d.reshape(bsz, t, d), g_in.reshape(1, d), w, w_dt_t, conv_w, conv_b.reshape(1, n_xbc),
      dt_bias.reshape(1, n_heads), dt_bias.reshape(n_heads, 1), a_log.reshape(1, n_heads), a_log.reshape(n_heads, 1),
      jnp.repeat(d_skip, hp).reshape(1, inner), norm_w.reshape(1, inner))
    return y.reshape(bsz * t, inner)


def _lru_kernel(x_ref, gn_ref, w_ref, cw_ref, cb_ref, wg_ref, bg_ref, lam_ref, y_ref, tail_ref, h_ref, a_ref, b_ref):
    @pl.when(pl.program_id(1) == 0)
    def _init():
        tail_ref[...] = jnp.zeros_like(tail_ref)
        h_ref[...] = jnp.zeros_like(h_ref)

    width = lam_ref.shape[1]
    xn = _rms(x_ref[0], gn_ref[...], NORM_EPS).astype(BF16)
    gate = jnp.dot(xn, w_ref[:, :width], preferred_element_type=F32)
    u = _causal_conv(jnp.dot(xn, w_ref[:, width:], preferred_element_type=F32), tail_ref, cw_ref[...]) + cb_ref[...]
    tt = u.shape[0]
    blocks, bw, _ = wg_ref.shape
    groups = (tt // SUBLANES, SUBLANES, bw)
    pos = lax.broadcasted_iota(jnp.int32, groups, 1)
    for n in range(blocks):
        ls = slice(n * bw, (n + 1) * bw)
        un = u[:, ls]
        gates = _dot(un, wg_ref[n])
        r = _sigmoid(gates[:, :bw] + bg_ref[0:1, ls])
        i = _sigmoid(gates[:, bw:] + bg_ref[1:2, ls])
        log_a = -LRU_C * r * _softplus(-lam_ref[:, ls])
        a = jnp.exp(log_a)
        b = jnp.sqrt(1.0 - jnp.exp(2.0 * log_a)) * (i * un)
        a, b = a.reshape(groups), b.reshape(groups)
        s = 1
        while s < SUBLANES:
            keep = pos >= s
            b = jnp.where(keep, a * pltpu.roll(b, s, 1) + b, b)
            a = jnp.where(keep, a * pltpu.roll(a, s, 1), a)
            s *= 2
        a_ref[:, ls] = a.reshape(tt, bw)
        b_ref[:, ls] = b.reshape(tt, bw)

    def carry(g, h):
        rows = pl.ds(pl.multiple_of(g * SUBLANES, SUBLANES), SUBLANES)
        hg = b_ref[rows, :] + a_ref[rows, :] * h
        b_ref[rows, :] = hg
        return hg[SUBLANES - 1:SUBLANES, :]

    h_ref[...] = lax.fori_loop(0, tt // SUBLANES, carry, h_ref[...], unroll=8)
    y_ref[0] = (_gelu_tanh(gate) * b_ref[...]).astype(y_ref.dtype)


def _lru_mixer(x2d, g_in, shape, w_in, conv_w, conv_b, w_gates, b_gates, lam):
    bsz, t, d = shape
    width = lam.shape[0]
    tile = pl.BlockSpec((1, TIME_TILE, d), lambda b, i: (b, i, 0))
    y = pl.pallas_call(
        _lru_kernel,
        grid=(bsz, t // TIME_TILE),
        in_specs=[tile, _resident((1, d)), _resident(w_in.shape), _resident(conv_w.shape), _resident((1, width)),
                  _resident(w_gates.shape), _resident(b_gates.shape), _resident((1, width))],
        out_specs=pl.BlockSpec((1, TIME_TILE, width), lambda b, i: (b, i, 0)),
        out_shape=jax.ShapeDtypeStruct((bsz, t, width), BF16),
        scratch_shapes=[pltpu.VMEM((SUBLANES, width), F32), pltpu.VMEM((1, width), F32),
                        pltpu.VMEM((TIME_TILE, width), F32), pltpu.VMEM((TIME_TILE, width), F32)],
        compiler_params=_params(2, 48),
        name="lru",
    )(x2d.reshape(bsz, t, d), g_in.reshape(1, d), w_in.astype(BF16), conv_w, conv_b.reshape(1, width),
      w_gates.astype(BF16), b_gates, lam.reshape(1, width))
    return y.reshape(bsz * t, width)


def kernel(x, norm_g, ffn_w_in, ffn_w_out, gdn_w_in, gdn_conv_w, gdn_a_log, gdn_dt_bias, gdn_norm_w, gdn_w_out, rwkv_mu, rwkv_w_rkv, rwkv_w0, rwkv_w1, rwkv_w2, rwkv_a0, rwkv_a1, rwkv_a2, rwkv_g1, rwkv_g2, rwkv_k_k, rwkv_k_a, rwkv_r_k, rwkv_ln_w, rwkv_ln_b, rwkv_w_out, ssd_w_in, ssd_conv_w, ssd_conv_b, ssd_dt_bias, ssd_a_log, ssd_d, ssd_norm_w, ssd_w_out, lru_w_in, lru_conv_w, lru_conv_b, lru_w_gates, lru_b_gates, lru_lam, lru_w_out):
    shape = x.shape
    bsz, t, d = shape
    depth = norm_g.shape[0]
    n_mixers = 4
    x2d = x.reshape(bsz * t, d)
    ffn_w_in, ffn_w_out = ffn_w_in.astype(BF16), ffn_w_out.astype(BF16)
    for i in range(depth):
        m, j = i % n_mixers, i // n_mixers
        g = norm_g[i]
        x2d = _ffn(x2d, g[0], g[1], ffn_w_in, ffn_w_out, (i, 0))
        if m == 0:
            y = _gdn_mixer(x2d, g[2], shape, gdn_w_in[j], gdn_conv_w[j], gdn_a_log[j], gdn_dt_bias[j], gdn_norm_w[j])
            w_o = gdn_w_out[j]
        elif m == 1:
            y = _rwkv_mixer(x2d, g[2], shape, rwkv_mu[j], rwkv_w_rkv[j], rwkv_w0[j], rwkv_w1[j], rwkv_w2[j],
                            rwkv_a0[j], rwkv_a1[j], rwkv_a2[j], rwkv_g1[j], rwkv_g2[j], rwkv_k_k[j], rwkv_k_a[j],
                            rwkv_r_k[j], rwkv_ln_w[j], rwkv_ln_b[j])
            w_o = rwkv_w_out[j]
        elif m == 2:
            y = _ssd_mixer(x2d, g[2], shape, ssd_w_in[j], ssd_conv_w[j], ssd_conv_b[j], ssd_dt_bias[j],
                           ssd_a_log[j], ssd_d[j], ssd_norm_w[j])
            w_o = ssd_w_out[j]
        else:
            y = _lru_mixer(x2d, g[2], shape, lru_w_in[j], lru_conv_w[j], lru_conv_b[j], lru_w_gates[j],
                           lru_b_gates[j], lru_lam[j])
            w_o = lru_w_out[j]
        x2d = _mix_out_ffn(y, w_o.astype(BF16), g[3], x2d, g[4], g[5], ffn_w_in, ffn_w_out, (i, 1))
    return x2d.reshape(shape)
```

```python
import functools

import jax
import jax.numpy as jnp
from jax import lax
from jax.experimental import pallas as pl
from jax.experimental.pallas import tpu as pltpu

F32 = jnp.float32
BF16 = jnp.bfloat16

V7X_VMEM_BYTES = 64 * 1024 * 1024
SUBLANES = 8
LANES = 128
MXU_WIDTH = 256

NORM_EPS = 1e-6
L2_EPS = 1e-6
SSD_NORM_EPS = 1e-5
RWKV_GN_EPS = 64e-5
LRU_C = 8.0

ROW_TILE = 512
TIME_TILE = 256
GDN_CHUNK = 64
RWKV_CHUNK = 64
SSD_CHUNK = 128
SSD_TIME_TILE = 512
SSD_GROUPS = 4
CONV_TAPS = 4


def _params(n_axes, vmem_mib):
    return pltpu.CompilerParams(
        dimension_semantics=("arbitrary",) * n_axes,
        vmem_limit_bytes=vmem_mib * 1024 * 1024,
    )


def _resident(shape):
    zeros = (0,) * len(shape)
    return pl.BlockSpec(shape, lambda *_: zeros, pipeline_mode=pl.Buffered(1))


def _slab(arr, idx):
    tail = tuple(arr.shape[len(idx):])
    at = tuple(idx) + (0,) * len(tail)
    return pl.BlockSpec((None,) * len(idx) + tail, lambda *_: at, pipeline_mode=pl.Buffered(1))


def _rms(x, g, eps):
    return x * lax.rsqrt(jnp.mean(x * x, axis=-1, keepdims=True) + eps) * g


def _sigmoid(x):
    return 1.0 / (1.0 + jnp.exp(-x))


def _silu(x):
    return x * _sigmoid(x)


def _softplus(x):
    return jnp.maximum(x, 0.0) + jnp.log1p(jnp.exp(-jnp.abs(x)))


def _gelu_tanh(x):
    return 0.5 * x * (1.0 + jnp.tanh(0.7978845608028654 * (x + 0.044715 * (x * x * x))))


def _dot(a, b):
    return jnp.dot(a.astype(BF16), b.astype(BF16), preferred_element_type=F32)


def _dot_nt(a, b):
    return lax.dot_general(a.astype(BF16), b.astype(BF16), (((1,), (1,)), ((), ())),
                           preferred_element_type=F32)


def _dot_tn(a, b):
    return lax.dot_general(a.astype(BF16), b.astype(BF16), (((0,), (0,)), ((), ())),
                           preferred_element_type=F32)


def _seg_cumsum(x, axis, seg):
    pos = lax.broadcasted_iota(jnp.int32, x.shape, axis) & (seg - 1)
    s = 1
    while s < seg:
        x = x + jnp.where(pos >= s, pltpu.roll(x, s, axis), 0.0)
        s *= 2
    return x


def _causal_conv(pre, tail_ref, w):
    tt = pre.shape[0]
    ext = jnp.concatenate([tail_ref[...], pre], axis=0)
    y = pre * w[CONV_TAPS - 1:CONV_TAPS, :]
    for k in range(1, CONV_TAPS):
        y = y + pltpu.roll(ext, k, 0)[SUBLANES:, :] * w[CONV_TAPS - 1 - k:CONV_TAPS - k, :]
    tail_ref[...] = pre[tt - SUBLANES:, :]
    return y


def _tri_inv(lows, block):
    n = lows[0].shape[0]
    n_sys = n // block
    row = lax.broadcasted_iota(jnp.int32, (n, n), 0)
    col = lax.broadcasted_iota(jnp.int32, (n, n), 1)
    on_diag = (row >> (block.bit_length() - 1)) == (col >> (block.bit_length() - 1))

    def to_diag(m_l):
        return jnp.where(on_diag, jnp.concatenate([m_l] * n_sys, axis=0), 0.0)

    row_l = lax.broadcasted_iota(jnp.int32, (block, n), 0)
    col_l = lax.broadcasted_iota(jnp.int32, (block, n), 1) & (block - 1)
    eye_l = jnp.where(row_l == col_l, 1.0, 0.0)
    pair_l = (row_l >> 1) == (col_l >> 1)
    t_ls = []
    for low in lows:
        low_l = low[:block]
        for s in range(1, n_sys):
            low_l = low_l + low[s * block:(s + 1) * block]
        t_ls.append(eye_l - jnp.where(pair_l, low_l, 0.0))
    b, sh = 2, 1
    while b < block:
        sub = ((row >> (sh + 1)) == (col >> (sh + 1))) & (((row >> sh) & 1) == 1) & (((col >> sh) & 1) == 0)
        xs = [_dot(t_l, jnp.where(sub, low, 0.0)) for t_l, low in zip(t_ls, lows)]
        t_ls = [t_l - _dot(x, to_diag(t_l)) for t_l, x in zip(t_ls, xs)]
        b, sh = 2 * b, sh + 1
    return [to_diag(t_l) for t_l in t_ls]


def _ffn_half_step(x, g0, g1, win_ref, wout_ref):
    d_ff = wout_ref.shape[0]
    mid = (d_ff // MXU_WIDTH // 2) * MXU_WIDTH
    xn = _rms(x, g0, NORM_EPS).astype(BF16)
    acc = None
    for lo, hi in ((0, mid), (mid, d_ff)) if 0 < mid < d_ff else ((0, d_ff),):
        gate = jnp.dot(xn, win_ref[:, lo:hi], preferred_element_type=F32)
        up = jnp.dot(xn, win_ref[:, d_ff + lo:d_ff + hi], preferred_element_type=F32)
        act = (_silu(gate) * up).astype(BF16)
        part = jnp.dot(act, wout_ref[lo:hi, :], preferred_element_type=F32)
        acc = part if acc is None else acc + part
    return x + 0.5 * _rms(acc, g1, NORM_EPS)


def _ffn_kernel(x_ref, g0_ref, g1_ref, win_ref, wout_ref, o_ref):
    o_ref[...] = _ffn_half_step(x_ref[...], g0_ref[...], g1_ref[...], win_ref, wout_ref)


def _ffn(x2d, g0, g1, w_in, w_out, idx):
    n, d = x2d.shape
    row = pl.BlockSpec((ROW_TILE, d), lambda i: (i, 0))
    return pl.pallas_call(
        _ffn_kernel,
        grid=(n // ROW_TILE,),
        in_specs=[row, _resident((1, d)), _resident((1, d)), _slab(w_in, idx), _slab(w_out, idx)],
        out_specs=row,
        out_shape=jax.ShapeDtypeStruct((n, d), F32),
        compiler_params=_params(1, 56),
        name="ffn",
    )(x2d, g0.reshape(1, d), g1.reshape(1, d), w_in, w_out)


def _mix_out_ffn_kernel(y_ref, wo_ref, go_ref, x_ref, g0_ref, g1_ref, win_ref, wout_ref, o_ref):
    h = jnp.dot(y_ref[...], wo_ref[...], preferred_element_type=F32)
    x = x_ref[...] + _rms(h, go_ref[...], NORM_EPS)
    o_ref[...] = _ffn_half_step(x, g0_ref[...], g1_ref[...], win_ref, wout_ref)


def _mix_out_ffn(y2d, w_o, g_o, x2d, g0, g1, w_in, w_out, idx):
    n, d = x2d.shape
    k = y2d.shape[1]
    row = pl.BlockSpec((ROW_TILE, d), lambda i: (i, 0))
    vec = _resident((1, d))
    return pl.pallas_call(
        _mix_out_ffn_kernel,
        grid=(n // ROW_TILE,),
        in_specs=[pl.BlockSpec((ROW_TILE, k), lambda i: (i, 0)), _resident(w_o.shape), vec, row, vec, vec,
                  _slab(w_in, idx), _slab(w_out, idx)],
        out_specs=row,
        out_shape=jax.ShapeDtypeStruct((n, d), F32),
        compiler_params=_params(1, 56),
        name="mix_out_ffn",
    )(y2d, w_o, g_o.reshape(1, d), x2d, g0.reshape(1, d), g1.reshape(1, d), w_in, w_out)


def _gdn_chunk_local(q, k, v, beta, gc_col, gc_row, causal, strict):
    decay = jnp.exp(jnp.where(causal, gc_col - gc_row, -jnp.inf))
    lower = jnp.where(strict, _dot_nt(k, k) * decay * beta, 0.0)
    eg = jnp.exp(gc_col)
    rhs = jnp.concatenate([v * beta, k * (beta * eg)], axis=1)
    return lower, rhs, _dot_nt(q, k) * decay, q * eg


def _gdn_kernel(x_ref, gn_ref, w_ref, wbat_ref, cw_ref, alc_ref, dtc_ref, alr_ref, dtr_ref, nw_ref, y_ref,
                tail_ref, s_ref, *, heads, dk, dv, chunk):
    @pl.when(pl.program_id(1) == 0)
    def _init():
        tail_ref[...] = jnp.zeros_like(tail_ref)
        s_ref[...] = jnp.zeros_like(s_ref)

    n_qkv = cw_ref.shape[1]
    o_k, o_v, o_ba = heads * dk, 2 * heads * dk, n_qkv + heads * dv
    xn = _rms(x_ref[0], gn_ref[...], NORM_EPS).astype(BF16)
    qkv = _silu(_causal_conv(jnp.dot(xn, w_ref[:, :n_qkv], preferred_element_type=F32), tail_ref, cw_ref[...]))
    z = jnp.dot(xn, w_ref[:, n_qkv:o_ba], preferred_element_type=F32)
    ba_c = jnp.dot(xn, w_ref[:, o_ba:], preferred_element_type=F32)
    ba_r = _dot_nt(wbat_ref[...], xn)
    beta_all = _sigmoid(ba_c)
    gc_cols = _seg_cumsum(-jnp.exp(alc_ref[...]) * _softplus(ba_c + dtc_ref[...]), 0, chunk)
    gc_rows = _seg_cumsum(-jnp.exp(alr_ref[...]) * _softplus(ba_r[heads:, :] + dtr_ref[...]), 1, chunk)
    tt = qkv.shape[0]
    shift = chunk.bit_length() - 1
    row = lax.broadcasted_iota(jnp.int32, (tt, tt), 0)
    col = lax.broadcasted_iota(jnp.int32, (tt, tt), 1)
    same_chunk = (row >> shift) == (col >> shift)
    causal = same_chunk & (row >= col)
    strict = same_chunk & (row > col)
    hds = []
    for h in range(heads):
        qh = qkv[:, h * dk:(h + 1) * dk]
        kh = qkv[:, o_k + h * dk:o_k + (h + 1) * dk]
        vh = qkv[:, o_v + h * dv:o_v + (h + 1) * dv]
        qh = qh * (lax.rsqrt(jnp.sum(qh * qh, axis=-1, keepdims=True) + L2_EPS) * dk ** -0.5)
        kh = kh * lax.rsqrt(jnp.sum(kh * kh, axis=-1, keepdims=True) + L2_EPS)
        beta = beta_all[:, h:h + 1]
        gc_col = gc_cols[:, heads + h:heads + h + 1]
        gc_row = gc_rows[h:h + 1, :]
        lower, rhs, qk, q_dec = _gdn_chunk_local(qh, kh, vh, beta, gc_col, gc_row, causal, strict)
        hds.append(dict(lower=lower, rhs=rhs, qk=qk, q_dec=q_dec, k=kh, gc=gc_col, state=s_ref[h],
                        v_new=[], o_state=[]))
    for hd, t_inv in zip(hds, _tri_inv([hd["lower"] for hd in hds], chunk)):
        sol = _dot(t_inv, hd["rhs"])
        hd.update(u=sol[:, :dv], w=sol[:, dv:])

    for c in range(tt // chunk):
        sl = slice(c * chunk, (c + 1) * chunk)
        for hd in hds:
            g_last = hd["gc"][(c + 1) * chunk - 1:(c + 1) * chunk, :]
            vn = hd["u"][sl] - _dot(hd["w"][sl], hd["state"])
            hd["o_state"].append(_dot(hd["q_dec"][sl], hd["state"]))
            hd["state"] = hd["state"] * jnp.exp(g_last) + _dot_tn(hd["k"][sl] * jnp.exp(g_last - hd["gc"][sl]), vn)
            hd["v_new"].append(vn)

    for h, hd in enumerate(hds):
        s_ref[h] = hd["state"]
        o = jnp.concatenate(hd["o_state"], axis=0) + _dot(hd["qk"], jnp.concatenate(hd["v_new"], axis=0))
        o = _rms(o, nw_ref[...], NORM_EPS) * _silu(z[:, h * dv:(h + 1) * dv])
        y_ref[0, :, h * dv:(h + 1) * dv] = o.astype(y_ref.dtype)


def _gdn_mixer(x2d, g_in, shape, w_in, conv_w, a_log, dt_bias, norm_w):
    bsz, t, d = shape
    heads = a_log.shape[0]
    dv = norm_w.shape[0]
    n_qkv = conv_w.shape[1]
    dk = (n_qkv - heads * dv) // (2 * heads)
    w = w_in.astype(BF16)
    w_ba_t = w[:, n_qkv + heads * dv:].T
    unused = jnp.zeros_like(a_log)
    tile = pl.BlockSpec((1, TIME_TILE, d), lambda b, i: (b, i, 0))
    y = pl.pallas_call(
        functools.partial(_gdn_kernel, heads=heads, dk=dk, dv=dv, chunk=GDN_CHUNK),
        grid=(bsz, t // TIME_TILE),
        in_specs=[tile, _resident((1, d)), _resident(w.shape), _resident(w_ba_t.shape), _resident(conv_w.shape),
                  _resident((1, 2 * heads)), _resident((1, 2 * heads)), _resident((heads, 1)), _resident((heads, 1)),
                  _resident((1, dv))],
        out_specs=pl.BlockSpec((1, TIME_TILE, heads * dv), lambda b, i: (b, i, 0)),
        out_shape=jax.ShapeDtypeStruct((bsz, t, heads * dv), BF16),
        scratch_shapes=[pltpu.VMEM((SUBLANES, n_qkv), F32), pltpu.VMEM((heads, dk, dv), F32)],
        compiler_params=_params(2, 56),
        name="gdn",
    )(x2d.reshape(bsz, t, d), g_in.reshape(1, d), w, w_ba_t, conv_w,
      jnp.concatenate([unused, a_log]).reshape(1, 2 * heads), jnp.concatenate([unused, dt_bias]).reshape(1, 2 * heads),
      a_log.reshape(heads, 1), dt_bias.reshape(heads, 1), norm_w.reshape(1, dv))
    return y.reshape(bsz * t, heads * dv)


def _rwkv_project(x_ref, g_ref, mu_ref, wrkv_ref, w0_ref, w1_ref, w2_ref, a0_ref, a1_ref, a2_ref,
                  g1_ref, g2_ref, r_ref, k_ref, v_ref, w_ref, a_ref, go_ref, prev_ref):
    hn = _rms(x_ref[0], g_ref[...], NORM_EPS)
    tm = hn.shape[0]
    ext = jnp.concatenate([prev_ref[...], hn], axis=0)
    xx = pltpu.roll(ext, 1, 0)[SUBLANES:, :] - hn
    prev_ref[...] = hn[tm - SUBLANES:, :]
    mu = mu_ref[...]

    def mix(i):
        return (hn + xx * mu[i:i + 1, :]).astype(BF16)

    r_ref[0] = jnp.dot(mix(0), wrkv_ref[0], preferred_element_type=F32)
    k_ref[0] = jnp.dot(mix(1), wrkv_ref[1], preferred_element_type=F32)
    v_ref[0] = jnp.dot(mix(2), wrkv_ref[2], preferred_element_type=F32)
    w_ref[0] = w0_ref[...] + _dot(jnp.tanh(jnp.dot(mix(3), w1_ref[...], preferred_element_type=F32)), w2_ref[...])
    a_ref[0] = a0_ref[...] + _dot(jnp.dot(mix(4), a1_ref[...], preferred_element_type=F32), a2_ref[...])
    go_ref[0] = _dot(_sigmoid(jnp.dot(mix(5), g1_ref[...], preferred_element_type=F32)), g2_ref[...])


def _rwkv_chunk_local(r, k, v, w_pre, a_pre, kk_w, ka_w, causal, strict, *, hd, chunk):
    tt, width = r.shape
    n_heads = width // hd
    lane = lax.broadcasted_iota(jnp.int32, (1, width), 1)
    head_of_lane = [(lane >= j * hd) & (lane < (j + 1) * hd) for j in range(n_heads)]

    def by_head(parts):
        out = parts[-1]
        for m, p in zip(head_of_lane[:-1], parts[:-1]):
            out = jnp.where(m, p, out)
        return out

    log_w = -jnp.exp(-_softplus(-w_pre) - 0.5)
    a = _sigmoid(a_pre)
    kk = k * kk_w
    kk = kk * lax.rsqrt(_head_sum(kk * kk, hd) + L2_EPS)
    k2 = k * (1.0 + (a - 1.0) * ka_w)
    cum = _seg_cumsum(log_w, 0, chunk)
    inv = jnp.exp(-cum)
    a_t = -kk * jnp.exp(cum - log_w)
    b_t = kk * a * inv
    k_t = k2 * inv
    r_t = r * jnp.exp(cum)

    bk = jnp.concatenate([b_t, k_t], axis=0)
    lows, rhs, y0_parts, p_rb = [], [], [], []
    for m in head_of_lane:
        ar = jnp.concatenate([jnp.where(m, a_t, 0.0), jnp.where(m, r_t, 0.0)], axis=0)
        g = _dot_nt(ar, bk)
        lows.append(jnp.where(strict, -g[:tt, :tt], 0.0))
        a_ak = jnp.where(strict, g[:tt, tt:], 0.0)
        p_rb.append(jnp.where(causal, g[tt:, :tt], 0.0))
        p_rk = jnp.where(causal, g[tt:, tt:], 0.0)
        rhs.append(jnp.concatenate([_dot(a_ak, v), a_t], axis=1))
        y0_parts.append(_dot(p_rk, v))
    return dict(lows=lows, rhs=rhs, y0=by_head(y0_parts), p_rb=p_rb, by_head=by_head,
                b_t=b_t, k_t=k_t, r_t=r_t, cum=cum, k2=k2, r=r, v=v)


def _head_sum(x, hd):
    lane = lax.broadcasted_iota(jnp.int32, (1, x.shape[1]), 1)
    tot = jnp.zeros_like(x)
    for j in range(x.shape[1] // hd):
        m = (lane >= j * hd) & (lane < (j + 1) * hd)
        tot = tot + jnp.where(m, jnp.sum(jnp.where(m, x, 0.0), axis=-1, keepdims=True), 0.0)
    return tot


def _rwkv_kernel(x_ref, gn_ref, mu_ref, wrkv_ref, w0_ref, w1_ref, w2_ref, a0_ref, a1_ref, a2_ref, g1_ref, g2_ref,
                 kk_ref, ka_ref, rk_ref, lnw_ref, lnb_ref, y_ref,
                 prev_ref, s_ref, r_ref, k_ref, v_ref, w_ref, a_ref, g_ref, *, hd, chunk):
    @pl.when(pl.program_id(1) == 0)
    def _init():
        prev_ref[...] = jnp.zeros_like(prev_ref)
        s_ref[...] = jnp.zeros_like(s_ref)

    _rwkv_project(x_ref, gn_ref, mu_ref, wrkv_ref, w0_ref, w1_ref, w2_ref, a0_ref, a1_ref, a2_ref, g1_ref, g2_ref,
                  r_ref, k_ref, v_ref, w_ref, a_ref, g_ref, prev_ref)
    _rwkv_recurrence(r_ref, k_ref, v_ref, w_ref, a_ref, g_ref, kk_ref, ka_ref, rk_ref, lnw_ref, lnb_ref,
                     y_ref, s_ref, hd=hd, chunk=chunk)


def _rwkv_recurrence(r_ref, k_ref, v_ref, w_ref, a_ref, g_ref, kk_ref, ka_ref, rk_ref, lnw_ref, lnb_ref,
                     y_ref, s_ref, *, hd, chunk):
    tt, width = r_ref.shape[1:]
    shift = chunk.bit_length() - 1
    rowi = lax.broadcasted_iota(jnp.int32, (tt, tt), 0)
    coli = lax.broadcasted_iota(jnp.int32, (tt, tt), 1)
    same_chunk = (rowi >> shift) == (coli >> shift)
    causal = same_chunk & (rowi >= coli)
    strict = same_chunk & (rowi > coli)
    srow = lax.broadcasted_iota(jnp.int32, (LANES, LANES), 0)
    scol = lax.broadcasted_iota(jnp.int32, (LANES, LANES), 1)
    same_head = None
    for j in range(LANES // hd):
        blk = (srow >= j * hd) & (srow < (j + 1) * hd) & (scol >= j * hd) & (scol < (j + 1) * hd)
        same_head = blk if same_head is None else (same_head | blk)

    groups = []
    for grp in range(width // LANES):
        ls = slice(grp * LANES, (grp + 1) * LANES)
        loc = _rwkv_chunk_local(r_ref[0, :, ls], k_ref[0, :, ls], v_ref[0, :, ls], w_ref[0, :, ls], a_ref[0, :, ls],
                                kk_ref[:, ls], ka_ref[:, ls], causal, strict, hd=hd, chunk=chunk)
        loc.update(state=s_ref[grp], us=[], rs=[])
        groups.append(loc)
    t_invs = _tri_inv([low for gp in groups for low in gp["lows"]], chunk)
    for gi, gp in enumerate(groups):
        n_h = len(gp["lows"])
        sols = [_dot(t_inv, rhs) for t_inv, rhs in zip(t_invs[gi * n_h:(gi + 1) * n_h], gp["rhs"])]
        gp.update(u0=gp["by_head"]([s[:, :LANES] for s in sols]), w_t=gp["by_head"]([s[:, LANES:] for s in sols]))

    for c in range(tt // chunk):
        sl = slice(c * chunk, (c + 1) * chunk)
        for gp in groups:
            u_c = gp["u0"][sl] + _dot_nt(gp["w_t"][sl], gp["state"])
            gp["rs"].append(_dot_nt(gp["r_t"][sl], gp["state"]))
            decay_end = jnp.exp(gp["cum"][(c + 1) * chunk - 1:(c + 1) * chunk, :])
            upd = gp["state"] + _dot_tn(u_c, gp["b_t"][sl]) + _dot_tn(gp["v"][sl], gp["k_t"][sl])
            gp["state"] = jnp.where(same_head, upd, 0.0) * decay_end
            gp["us"].append(u_c)

    for grp, gp in enumerate(groups):
        ls = slice(grp * LANES, (grp + 1) * LANES)
        s_ref[grp] = gp["state"]
        u = jnp.concatenate(gp["us"], axis=0)
        y = jnp.concatenate(gp["rs"], axis=0) + gp["y0"] + gp["by_head"]([_dot(p, u) for p in gp["p_rb"]])
        mean = _head_sum(y, hd) * (1.0 / hd)
        yc = y - mean
        var = _head_sum(yc * yc, hd) * (1.0 / hd)
        y = yc * lax.rsqrt(var + RWKV_GN_EPS) * lnw_ref[:, ls] + lnb_ref[:, ls]
        bonus = _head_sum(gp["r"] * gp["k2"] * rk_ref[:, ls], hd) * gp["v"]
        y_ref[0, :, ls] = ((y + bonus) * g_ref[0, :, ls]).astype(y_ref.dtype)


def _rwkv_mixer(x2d, g_in, shape, mu, w_rkv, w0, w1, w2, a0, a1, a2, g1, g2, k_k, k_a, r_k, ln_w, ln_b):
    bsz, t, d = shape
    hd = r_k.shape[1]
    tile = pl.BlockSpec((1, TIME_TILE, d), lambda b, i: (b, i, 0))
    vec = _resident((1, d))
    proj = pltpu.VMEM((1, TIME_TILE, d), F32)
    y = pl.pallas_call(
        functools.partial(_rwkv_kernel, hd=hd, chunk=RWKV_CHUNK),
        grid=(bsz, t // TIME_TILE),
        in_specs=[tile, vec, _resident(mu.shape), _resident(w_rkv.shape), vec, _resident(w1.shape),
                  _resident(w2.shape), vec, _resident(a1.shape), _resident(a2.shape), _resident(g1.shape),
                  _resident(g2.shape)] + [vec] * 5,
        out_specs=tile,
        out_shape=jax.ShapeDtypeStruct((bsz, t, d), BF16),
        scratch_shapes=[pltpu.VMEM((SUBLANES, d), F32), pltpu.VMEM((d // LANES, LANES, LANES), F32)] + [proj] * 6,
        compiler_params=_params(2, 56),
        name="rwkv",
    )(x2d.reshape(bsz, t, d), g_in.reshape(1, d), mu, w_rkv.astype(BF16), w0.reshape(1, d), w1.astype(BF16),
      w2.astype(BF16), a0.reshape(1, d), a1.astype(BF16), a2.astype(BF16), g1.astype(BF16), g2.astype(BF16),
      k_k.reshape(1, d), k_a.reshape(1, d), r_k.reshape(1, d), ln_w.reshape(1, d), ln_b.reshape(1, d))
    return y.reshape(bsz * t, d)


def _ssd_kernel(x_ref, gn_ref, w_ref, wdtt_ref, cw_ref, cb_ref, dtbc_ref, dtbr_ref, alc_ref, alr_ref, dskip_ref, nw_ref,
                y_ref, tail_ref, h_ref, *, groups, gh, hp, ns, chunk):
    @pl.when(pl.program_id(1) == 0)
    def _init():
        tail_ref[...] = jnp.zeros_like(tail_ref)
        h_ref[...] = jnp.zeros_like(h_ref)

    gw = gh * hp
    inner = groups * gw
    n_xbc = cw_ref.shape[1]
    xn = _rms(x_ref[0], gn_ref[...], NORM_EPS).astype(BF16)
    z = jnp.dot(xn, w_ref[:, :inner], preferred_element_type=F32)
    xbc = jnp.dot(xn, w_ref[:, inner:inner + n_xbc], preferred_element_type=F32)
    xbc = _silu(_causal_conv(xbc, tail_ref, cw_ref[...]) + cb_ref[...])
    dt_c = _softplus(jnp.dot(xn, w_ref[:, inner + n_xbc:], preferred_element_type=F32) + dtbc_ref[...])
    dt_r = _softplus(_dot_nt(wdtt_ref[...], xn) + dtbr_ref[...])
    acs_c = _seg_cumsum(dt_c * -jnp.exp(alc_ref[...]), 0, chunk)
    acs_r = _seg_cumsum(dt_r * -jnp.exp(alr_ref[...]), 1, chunk)
    tt = xbc.shape[0]

    n_heads = groups * gh
    head_row = lax.broadcasted_iota(jnp.int32, (n_heads, inner), 0)
    head_of_lane = lax.broadcasted_iota(jnp.int32, (n_heads, inner), 1) >> (hp.bit_length() - 1)
    spread = jnp.where(head_row == head_of_lane, 1.0, 0.0).astype(BF16)

    def expand(cols):
        rows = cols.shape[0]
        hi = cols.astype(BF16)
        rest = cols - hi.astype(F32)
        mid = rest.astype(BF16)
        lo = (rest - mid.astype(F32)).astype(BF16)
        parts = jnp.dot(jnp.concatenate([hi, mid, lo], axis=0), spread, preferred_element_type=F32)
        return parts[:rows] + parts[rows:2 * rows] + parts[2 * rows:]

    rowi = lax.broadcasted_iota(jnp.int32, (chunk, chunk), 0)
    coli = lax.broadcasted_iota(jnp.int32, (chunk, chunk), 1)
    causal = rowi >= coli
    pair_lo = lax.broadcasted_iota(jnp.int32, (1, 2 * hp), 1) < hp

    states = [h_ref[g] for g in range(groups)]
    ys = [[] for _ in range(groups)]
    for c in range(tt // chunk):
        sl = slice(c * chunk, (c + 1) * chunk)
        ac_c = acs_c[sl]
        ac_r = acs_r[:, sl]
        tot = ac_c[chunk - 1:chunk, :]
        dt_x = expand(dt_c[sl])
        in_decay_x = expand(jnp.exp(tot - ac_c))
        out_decay_x = expand(jnp.exp(ac_c))
        for g in range(groups):
            first = g * gh
            gl = slice(g * gw, (g + 1) * gw)
            xc = xbc[sl, gl]
            bc = xbc[sl, inner + g * ns:inner + (g + 1) * ns]
            cc = xbc[sl, inner + groups * ns + g * ns:inner + groups * ns + (g + 1) * ns]
            xdt = xc * dt_x[:, gl]
            cb = _dot_nt(cc, bc)
            y_pairs = []
            for p in range(gh // 2):
                xp = xdt[:, 2 * p * hp:2 * (p + 1) * hp]
                halves = []
                for e in (first + 2 * p, first + 2 * p + 1):
                    seg = jnp.where(causal, ac_c[:, e:e + 1] - ac_r[e:e + 1, :], -jnp.inf)
                    halves.append(_dot(cb * jnp.exp(seg), xp))
                y_pairs.append(jnp.where(pair_lo, halves[0], halves[1]))
            y_diag = jnp.concatenate(y_pairs, axis=1)
            y_off = _dot(cc, states[g]) * out_decay_x[:, gl]
            new = _dot_tn(bc, xdt * in_decay_x[:, gl])
            states[g] = states[g] * out_decay_x[chunk - 1:chunk, gl] + new
            ys[g].append(y_diag + y_off + xc * dskip_ref[:, gl])
    for g in range(groups):
        h_ref[g] = states[g]
        ls = slice(g * gw, (g + 1) * gw)
        y = jnp.concatenate(ys[g], axis=0) * _silu(z[:, ls])
        y_ref[0, :, ls] = _rms(y, nw_ref[:, ls], SSD_NORM_EPS).astype(y_ref.dtype)


def _ssd_mixer(x2d, g_in, shape, w_in, conv_w, conv_b, dt_bias, a_log, d_skip, norm_w):
    bsz, t, d = shape
    n_heads = a_log.shape[0]
    inner = norm_w.shape[0]
    hp = inner // n_heads
    n_xbc = conv_w.shape[1]
    groups = SSD_GROUPS
    ns = (n_xbc - inner) // (2 * groups)
    gh = n_heads // groups
    w = w_in.astype(BF16)
    w_dt_t = w[:, inner + n_xbc:].T
    tile = SSD_TIME_TILE
    y = pl.pallas_call(
        functools.partial(_ssd_kernel, groups=groups, gh=gh, hp=hp, ns=ns, chunk=SSD_CHUNK),
        grid=(bsz, t // tile),
        in_specs=[pl.BlockSpec((1, tile, d), lambda b, i: (b, i, 0)), _resident((1, d)), _resident(w.shape),
                  _resident(w_dt_t.shape), _resident(conv_w.shape), _resident((1, n_xbc)),
                  _resident((1, n_heads)), _resident((n_heads, 1)), _resident((1, n_heads)), _resident((n_heads, 1)),
                  _resident((1, inner)), _resident((1, inner))],
        out_specs=pl.BlockSpec((1, tile, inner), lambda b, i: (b, i, 0)),
        out_shape=jax.ShapeDtypeStruct((bsz, t, inner), BF16),
        scratch_shapes=[pltpu.VMEM((SUBLANES, n_xbc), F32), pltpu.VMEM((groups, ns, inner // groups), F32)],
        compiler_params=_params(2, 56),
        name="ssd",
    )(x2d.reshape(bsz, t, d), g_in.reshape(1, d), w, w_dt_t, conv_w, conv_b.reshape(1, n_xbc),
      dt_bias.reshape(1, n_heads), dt_bias.reshape(n_heads, 1), a_log.reshape(1, n_heads), a_log.reshape(n_heads, 1),
      jnp.repeat(d_skip, hp).reshape(1, inner), norm_w.reshape(1, inner))
    return y.reshape(bsz * t, inner)


def _lru_kernel(x_ref, gn_ref, w_ref, cw_ref, cb_ref, wg_ref, bg_ref, lam_ref, y_ref, tail_ref, h_ref, a_ref, b_ref):
    @pl.when(pl.program_id(1) == 0)
    def _init():
        tail_ref[...] = jnp.zeros_like(tail_ref)
        h_ref[...] = jnp.zeros_like(h_ref)

    width = lam_ref.shape[1]
    xn = _rms(x_ref[0], gn_ref[...], NORM_EPS).astype(BF16)
    gate = jnp.dot(xn, w_ref[:, :width], preferred_element_type=F32)
    u = _causal_conv(jnp.dot(xn, w_ref[:, width:], preferred_element_type=F32), tail_ref, cw_ref[...]) + cb_ref[...]
    tt = u.shape[0]
    blocks, bw, _ = wg_ref.shape
    groups = (tt // SUBLANES, SUBLANES, bw)
    pos = lax.broadcasted_iota(jnp.int32, groups, 1)
    for n in range(blocks):
        ls = slice(n * bw, (n + 1) * bw)
        un = u[:, ls]
        gates = _dot(un, wg_ref[n])
        r = _sigmoid(gates[:, :bw] + bg_ref[0:1, ls])
        i = _sigmoid(gates[:, bw:] + bg_ref[1:2, ls])
        log_a = -LRU_C * r * _softplus(-lam_ref[:, ls])
        a = jnp.exp(log_a)
        b = jnp.sqrt(1.0 - jnp.exp(2.0 * log_a)) * (i * un)
        a, b = a.reshape(groups), b.reshape(groups)
        s = 1
        while s < SUBLANES:
            keep = pos >= s
            b = jnp.where(keep, a * pltpu.roll(b, s, 1) + b, b)
            a = jnp.where(keep, a * pltpu.roll(a, s, 1), a)
            s *= 2
        a_ref[:, ls] = a.reshape(tt, bw)
        b_ref[:, ls] = b.reshape(tt, bw)

    def carry(g, h):
        rows = pl.ds(pl.multiple_of(g * SUBLANES, SUBLANES), SUBLANES)
        hg = b_ref[rows, :] + a_ref[rows, :] * h
        b_ref[rows, :] = hg
        return hg[SUBLANES - 1:SUBLANES, :]

    h_ref[...] = lax.fori_loop(0, tt // SUBLANES, carry, h_ref[...], unroll=8)
    y_ref[0] = (_gelu_tanh(gate) * b_ref[...]).astype(y_ref.dtype)


def _lru_mixer(x2d, g_in, shape, w_in, conv_w, conv_b, w_gates, b_gates, lam):
    bsz, t, d = shape
    width = lam.shape[0]
    tile = pl.BlockSpec((1, TIME_TILE, d), lambda b, i: (b, i, 0))
    y = pl.pallas_call(
        _lru_kernel,
        grid=(bsz, t // TIME_TILE),
        in_specs=[tile, _resident((1, d)), _resident(w_in.shape), _resident(conv_w.shape), _resident((1, width)),
                  _resident(w_gates.shape), _resident(b_gates.shape), _resident((1, width))],
        out_specs=pl.BlockSpec((1, TIME_TILE, width), lambda b, i: (b, i, 0)),
        out_shape=jax.ShapeDtypeStruct((bsz, t, width), BF16),
        scratch_shapes=[pltpu.VMEM((SUBLANES, width), F32), pltpu.VMEM((1, width), F32),
                        pltpu.VMEM((TIME_TILE, width), F32), pltpu.VMEM((TIME_TILE, width), F32)],
        compiler_params=_params(2, 48),
        name="lru",
    )(x2d.reshape(bsz, t, d), g_in.reshape(1, d), w_in.astype(BF16), conv_w, conv_b.reshape(1, width),
      w_gates.astype(BF16), b_gates, lam.reshape(1, width))
    return y.reshape(bsz * t, width)


def kernel(x, norm_g, ffn_w_in, ffn_w_out, gdn_w_in, gdn_conv_w, gdn_a_log, gdn_dt_bias, gdn_norm_w, gdn_w_out, rwkv_mu, rwkv_w_rkv, rwkv_w0, rwkv_w1, rwkv_w2, rwkv_a0, rwkv_a1, rwkv_a2, rwkv_g1, rwkv_g2, rwkv_k_k, rwkv_k_a, rwkv_r_k, rwkv_ln_w, rwkv_ln_b, rwkv_w_out, ssd_w_in, ssd_conv_w, ssd_conv_b, ssd_dt_bias, ssd_a_log, ssd_d, ssd_norm_w, ssd_w_out, lru_w_in, lru_conv_w, lru_conv_b, lru_w_gates, lru_b_gates, lru_lam, lru_w_out):
    shape = x.shape
    bsz, t, d = shape
    depth = norm_g.shape[0]
    n_mixers = 4
    x2d = x.reshape(bsz * t, d)
    ffn_w_in, ffn_w_out = ffn_w_in.astype(BF16), ffn_w_out.astype(BF16)
    for i in range(depth):
        m, j = i % n_mixers, i // n_mixers
        g = norm_g[i]
        x2d = _ffn(x2d, g[0], g[1], ffn_w_in, ffn_w_out, (i, 0))
        if m == 0:
            y = _gdn_mixer(x2d, g[2], shape, gdn_w_in[j], gdn_conv_w[j], gdn_a_log[j], gdn_dt_bias[j], gdn_norm_w[j])
            w_o = gdn_w_out[j]
        elif m == 1:
            y = _rwkv_mixer(x2d, g[2], shape, rwkv_mu[j], rwkv_w_rkv[j], rwkv_w0[j], rwkv_w1[j], rwkv_w2[j],
                            rwkv_a0[j], rwkv_a1[j], rwkv_a2[j], rwkv_g1[j], rwkv_g2[j], rwkv_k_k[j], rwkv_k_a[j],
                            rwkv_r_k[j], rwkv_ln_w[j], rwkv_ln_b[j])
            w_o = rwkv_w_out[j]
        elif m == 2:
            y = _ssd_mixer(x2d, g[2], shape, ssd_w_in[j], ssd_conv_w[j], ssd_conv_b[j], ssd_dt_bias[j],
                           ssd_a_log[j], ssd_d[j], ssd_norm_w[j])
            w_o = ssd_w_out[j]
        else:
            y = _lru_mixer(x2d, g[2], shape, lru_w_in[j], lru_conv_w[j], lru_conv_b[j], lru_w_gates[j],
                           lru_b_gates[j], lru_lam[j])
            w_o = lru_w_out[j]
        x2d = _mix_out_ffn(y, w_o.astype(BF16), g[3], x2d, g[4], g[5], ffn_w_in, ffn_w_out, (i, 1))
    return x2d.reshape(shape)
```

```python
import functools

import jax
import jax.numpy as jnp
from jax import lax
from jax.experimental import pallas as pl
from jax.experimental.pallas import tpu as pltpu

F32 = jnp.float32
BF16 = jnp.bfloat16

V7X_VMEM_BYTES = 64 * 1024 * 1024
SUBLANES = 8
LANES = 128
MXU_WIDTH = 256

NORM_EPS = 1e-6
L2_EPS = 1e-6
SSD_NORM_EPS = 1e-5
RWKV_GN_EPS = 64e-5
LRU_C = 8.0

ROW_TILE = 512
TIME_TILE = 256
GDN_CHUNK = 64
RWKV_CHUNK = 64
SSD_CHUNK = 128
SSD_TIME_TILE = 512
SSD_GROUPS = 4
CONV_TAPS = 4
WEIGHT_PIECES = 8


def _params(n_axes, vmem_mib):
    return pltpu.CompilerParams(
        dimension_semantics=("arbitrary",) * n_axes,
        vmem_limit_bytes=vmem_mib * 1024 * 1024,
    )


def _resident(shape):
    zeros = (0,) * len(shape)
    return pl.BlockSpec(shape, lambda *_: zeros, pipeline_mode=pl.Buffered(1))


def _rms(x, g, eps):
    return x * lax.rsqrt(jnp.mean(x * x, axis=-1, keepdims=True) + eps) * g


def _sigmoid(x):
    return 1.0 / (1.0 + jnp.exp(-x))


def _silu(x):
    return x * _sigmoid(x)


def _softplus(x):
    return jnp.maximum(x, 0.0) + jnp.log1p(jnp.exp(-jnp.abs(x)))


def _gelu_tanh(x):
    return 0.5 * x * (1.0 + jnp.tanh(0.7978845608028654 * (x + 0.044715 * (x * x * x))))


def _dot(a, b):
    return jnp.dot(a.astype(BF16), b.astype(BF16), preferred_element_type=F32)


def _dot_nt(a, b):
    return lax.dot_general(a.astype(BF16), b.astype(BF16), (((1,), (1,)), ((), ())),
                           preferred_element_type=F32)


def _dot_tn(a, b):
    return lax.dot_general(a.astype(BF16), b.astype(BF16), (((0,), (0,)), ((), ())),
                           preferred_element_type=F32)


def _seg_cumsum(x, axis, seg):
    pos = lax.broadcasted_iota(jnp.int32, x.shape, axis) & (seg - 1)
    s = 1
    while s < seg:
        x = x + jnp.where(pos >= s, pltpu.roll(x, s, axis), 0.0)
        s *= 2
    return x


def _causal_conv(pre, tail_ref, w):
    tt = pre.shape[0]
    ext = jnp.concatenate([tail_ref[...], pre], axis=0)
    y = pre * w[CONV_TAPS - 1:CONV_TAPS, :]
    for k in range(1, CONV_TAPS):
        y = y + pltpu.roll(ext, k, 0)[SUBLANES:, :] * w[CONV_TAPS - 1 - k:CONV_TAPS - k, :]
    tail_ref[...] = pre[tt - SUBLANES:, :]
    return y


def _tri_inv(lows, block):
    n = lows[0].shape[0]
    n_sys = n // block
    row = lax.broadcasted_iota(jnp.int32, (n, n), 0)
    col = lax.broadcasted_iota(jnp.int32, (n, n), 1)
    on_diag = (row >> (block.bit_length() - 1)) == (col >> (block.bit_length() - 1))

    def to_diag(m_l):
        return jnp.where(on_diag, jnp.concatenate([m_l] * n_sys, axis=0), 0.0)

    row_l = lax.broadcasted_iota(jnp.int32, (block, n), 0)
    col_l = lax.broadcasted_iota(jnp.int32, (block, n), 1) & (block - 1)
    eye_l = jnp.where(row_l == col_l, 1.0, 0.0)
    pair_l = (row_l >> 1) == (col_l >> 1)
    t_ls = []
    for low in lows:
        low_l = low[:block]
        for s in range(1, n_sys):
            low_l = low_l + low[s * block:(s + 1) * block]
        t_ls.append(eye_l - jnp.where(pair_l, low_l, 0.0))
    b, sh = 2, 1
    while b < block:
        sub = ((row >> (sh + 1)) == (col >> (sh + 1))) & (((row >> sh) & 1) == 1) & (((col >> sh) & 1) == 0)
        xs = [_dot(t_l, jnp.where(sub, low, 0.0)) for t_l, low in zip(t_ls, lows)]
        t_ls = [t_l - _dot(x, to_diag(t_l)) for t_l, x in zip(t_ls, xs)]
        b, sh = 2 * b, sh + 1
    return [to_diag(t_l) for t_l in t_ls]


def _ffn_half_step(x, g0, g1, win_ref, wout_ref):
    d_ff = wout_ref.shape[0]
    mid = (d_ff // MXU_WIDTH // 2) * MXU_WIDTH
    xn = _rms(x, g0, NORM_EPS).astype(BF16)
    acc = None
    for lo, hi in ((0, mid), (mid, d_ff)) if 0 < mid < d_ff else ((0, d_ff),):
        gate = jnp.dot(xn, win_ref[:, lo:hi], preferred_element_type=F32)
        up = jnp.dot(xn, win_ref[:, d_ff + lo:d_ff + hi], preferred_element_type=F32)
        act = (_silu(gate) * up).astype(BF16)
        part = jnp.dot(act, wout_ref[lo:hi, :], preferred_element_type=F32)
        acc = part if acc is None else acc + part
    return x + _rms(acc, 0.5 * g1, NORM_EPS)


def _stage_weight(src_hbm, idx, dst_ref, stage_ref, sem_ref):
    rows = stage_ref.shape[1]

    def copy(c):
        piece = src_hbm.at[idx[0], idx[1], pl.ds(c * rows, rows)]
        return pltpu.make_async_copy(piece, stage_ref.at[c % 2], sem_ref.at[c % 2])

    n_pieces = dst_ref.shape[0] // rows
    copy(0).start()
    for c in range(n_pieces):
        if c + 1 < n_pieces:
            copy(c + 1).start()
        copy(c).wait()
        dst_ref[c * rows:(c + 1) * rows, :] = stage_ref[c % 2].astype(BF16)


def _ffn_weight_scratch(w_in, w_out):
    (d, f2), (f, _) = w_in.shape[2:], w_out.shape[2:]
    return [pltpu.VMEM((d, f2), BF16), pltpu.VMEM((f, d), BF16),
            pltpu.VMEM((2, d // WEIGHT_PIECES, f2), F32), pltpu.VMEM((2, f // WEIGHT_PIECES, d), F32),
            pltpu.SemaphoreType.DMA((2,)), pltpu.SemaphoreType.DMA((2,))]


def _load_ffn_weights(win_hbm, wout_hbm, idx, win_ref, wout_ref, sin_ref, sout_ref, semi_ref, semo_ref):
    @pl.when(pl.program_id(0) == 0)
    def _load():
        _stage_weight(win_hbm, idx, win_ref, sin_ref, semi_ref)
        _stage_weight(wout_hbm, idx, wout_ref, sout_ref, semo_ref)


def _ffn_kernel(x_ref, g0_ref, g1_ref, win_hbm, wout_hbm, o_ref, win_ref, wout_ref, *stage, idx):
    _load_ffn_weights(win_hbm, wout_hbm, idx, win_ref, wout_ref, *stage)
    o_ref[...] = _ffn_half_step(x_ref[...], g0_ref[...], g1_ref[...], win_ref, wout_ref)


def _ffn(x2d, g0, g1, w_in, w_out, idx):
    n, d = x2d.shape
    row = pl.BlockSpec((ROW_TILE, d), lambda i: (i, 0))
    hbm = pl.BlockSpec(memory_space=pl.ANY)
    return pl.pallas_call(
        functools.partial(_ffn_kernel, idx=idx),
        grid=(n // ROW_TILE,),
        in_specs=[row, _resident((1, d)), _resident((1, d)), hbm, hbm],
        out_specs=row,
        out_shape=jax.ShapeDtypeStruct((n, d), F32),
        scratch_shapes=_ffn_weight_scratch(w_in, w_out),
        compiler_params=_params(1, 56),
        name="ffn",
    )(x2d, g0.reshape(1, d), g1.reshape(1, d), w_in, w_out)


def _mix_out_ffn_kernel(y_ref, wo_ref, go_ref, x_ref, g0_ref, g1_ref, win_hbm, wout_hbm, o_ref, win_ref, wout_ref,
                        *stage, idx):
    _load_ffn_weights(win_hbm, wout_hbm, idx, win_ref, wout_ref, *stage)
    h = jnp.dot(y_ref[...], wo_ref[...], preferred_element_type=F32)
    x = x_ref[...] + _rms(h, go_ref[...], NORM_EPS)
    o_ref[...] = _ffn_half_step(x, g0_ref[...], g1_ref[...], win_ref, wout_ref)


def _mix_out_ffn(y2d, w_o, g_o, x2d, g0, g1, w_in, w_out, idx):
    n, d = x2d.shape
    k = y2d.shape[1]
    row = pl.BlockSpec((ROW_TILE, d), lambda i: (i, 0))
    vec = _resident((1, d))
    hbm = pl.BlockSpec(memory_space=pl.ANY)
    return pl.pallas_call(
        functools.partial(_mix_out_ffn_kernel, idx=idx),
        grid=(n // ROW_TILE,),
        in_specs=[pl.BlockSpec((ROW_TILE, k), lambda i: (i, 0)), _resident(w_o.shape), vec, row, vec, vec, hbm, hbm],
        out_specs=row,
        out_shape=jax.ShapeDtypeStruct((n, d), F32),
        scratch_shapes=_ffn_weight_scratch(w_in, w_out),
        compiler_params=_params(1, 56),
        name="mix_out_ffn",
    )(y2d, w_o, g_o.reshape(1, d), x2d, g0.reshape(1, d), g1.reshape(1, d), w_in, w_out)


def _gdn_chunk_local(q, k, v, beta, gc_col, gc_row, causal, strict):
    decay = jnp.exp(jnp.where(causal, gc_col - gc_row, -jnp.inf))
    lower = jnp.where(strict, _dot_nt(k, k) * decay * beta, 0.0)
    eg = jnp.exp(gc_col)
    rhs = jnp.concatenate([v * beta, k * (beta * eg)], axis=1)
    return lower, rhs, _dot_nt(q, k) * decay, q * eg


def _gdn_kernel(x_ref, gn_ref, w_ref, wbat_ref, cw_ref, alc_ref, dtc_ref, alr_ref, dtr_ref, nw_ref, y_ref,
                tail_ref, s_ref, *, heads, dk, dv, chunk):
    @pl.when(pl.program_id(1) == 0)
    def _init():
        tail_ref[...] = jnp.zeros_like(tail_ref)
        s_ref[...] = jnp.zeros_like(s_ref)

    n_qkv = cw_ref.shape[1]
    o_k, o_v, o_ba = heads * dk, 2 * heads * dk, n_qkv + heads * dv
    xn = _rms(x_ref[0], gn_ref[...], NORM_EPS).astype(BF16)
    qkv = _silu(_causal_conv(jnp.dot(xn, w_ref[:, :n_qkv], preferred_element_type=F32), tail_ref, cw_ref[...]))
    z = jnp.dot(xn, w_ref[:, n_qkv:o_ba], preferred_element_type=F32)
    ba_c = jnp.dot(xn, w_ref[:, o_ba:], preferred_element_type=F32)
    ba_r = _dot_nt(wbat_ref[...], xn)
    beta_all = _sigmoid(ba_c)
    gc_cols = _seg_cumsum(-jnp.exp(alc_ref[...]) * _softplus(ba_c + dtc_ref[...]), 0, chunk)
    gc_rows = _seg_cumsum(-jnp.exp(alr_ref[...]) * _softplus(ba_r[heads:, :] + dtr_ref[...]), 1, chunk)
    tt = qkv.shape[0]
    shift = chunk.bit_length() - 1
    row = lax.broadcasted_iota(jnp.int32, (tt, tt), 0)
    col = lax.broadcasted_iota(jnp.int32, (tt, tt), 1)
    same_chunk = (row >> shift) == (col >> shift)
    causal = same_chunk & (row >= col)
    strict = same_chunk & (row > col)
    hds = []
    for h in range(heads):
        qh = qkv[:, h * dk:(h + 1) * dk]
        kh = qkv[:, o_k + h * dk:o_k + (h + 1) * dk]
        vh = qkv[:, o_v + h * dv:o_v + (h + 1) * dv]
        qh = qh * (lax.rsqrt(jnp.sum(qh * qh, axis=-1, keepdims=True) + L2_EPS) * dk ** -0.5)
        kh = kh * lax.rsqrt(jnp.sum(kh * kh, axis=-1, keepdims=True) + L2_EPS)
        beta = beta_all[:, h:h + 1]
        gc_col = gc_cols[:, heads + h:heads + h + 1]
        gc_row = gc_rows[h:h + 1, :]
        lower, rhs, qk, q_dec = _gdn_chunk_local(qh, kh, vh, beta, gc_col, gc_row, causal, strict)
        hds.append(dict(lower=lower, rhs=rhs, qk=qk, q_dec=q_dec, k=kh, gc=gc_col, state=s_ref[h],
                        v_new=[], o_state=[]))
    for hd, t_inv in zip(hds, _tri_inv([hd["lower"] for hd in hds], chunk)):
        sol = _dot(t_inv, hd["rhs"])
        hd.update(u=sol[:, :dv], w=sol[:, dv:])

    for c in range(tt // chunk):
        sl = slice(c * chunk, (c + 1) * chunk)
        for hd in hds:
            g_last = hd["gc"][(c + 1) * chunk - 1:(c + 1) * chunk, :]
            vn = hd["u"][sl] - _dot(hd["w"][sl], hd["state"])
            hd["o_state"].append(_dot(hd["q_dec"][sl], hd["state"]))
            hd["state"] = hd["state"] * jnp.exp(g_last) + _dot_tn(hd["k"][sl] * jnp.exp(g_last - hd["gc"][sl]), vn)
            hd["v_new"].append(vn)

    for h, hd in enumerate(hds):
        s_ref[h] = hd["state"]
        o = jnp.concatenate(hd["o_state"], axis=0) + _dot(hd["qk"], jnp.concatenate(hd["v_new"], axis=0))
        o = _rms(o, nw_ref[...], NORM_EPS) * _silu(z[:, h * dv:(h + 1) * dv])
        y_ref[0, :, h * dv:(h + 1) * dv] = o.astype(y_ref.dtype)


def _gdn_mixer(x2d, g_in, shape, w_in, conv_w, a_log, dt_bias, norm_w):
    bsz, t, d = shape
    heads = a_log.shape[0]
    dv = norm_w.shape[0]
    n_qkv = conv_w.shape[1]
    dk = (n_qkv - heads * dv) // (2 * heads)
    w = w_in.astype(BF16)
    w_ba_t = w[:, n_qkv + heads * dv:].T
    unused = jnp.zeros_like(a_log)
    tile = pl.BlockSpec((1, TIME_TILE, d), lambda b, i: (b, i, 0))
    y = pl.pallas_call(
        functools.partial(_gdn_kernel, heads=heads, dk=dk, dv=dv, chunk=GDN_CHUNK),
        grid=(bsz, t // TIME_TILE),
        in_specs=[tile, _resident((1, d)), _resident(w.shape), _resident(w_ba_t.shape), _resident(conv_w.shape),
                  _resident((1, 2 * heads)), _resident((1, 2 * heads)), _resident((heads, 1)), _resident((heads, 1)),
                  _resident((1, dv))],
        out_specs=pl.BlockSpec((1, TIME_TILE, heads * dv), lambda b, i: (b, i, 0)),
        out_shape=jax.ShapeDtypeStruct((bsz, t, heads * dv), BF16),
        scratch_shapes=[pltpu.VMEM((SUBLANES, n_qkv), F32), pltpu.VMEM((heads, dk, dv), F32)],
        compiler_params=_params(2, 56),
        name="gdn",
    )(x2d.reshape(bsz, t, d), g_in.reshape(1, d), w, w_ba_t, conv_w,
      jnp.concatenate([unused, a_log]).reshape(1, 2 * heads), jnp.concatenate([unused, dt_bias]).reshape(1, 2 * heads),
      a_log.reshape(heads, 1), dt_bias.reshape(heads, 1), norm_w.reshape(1, dv))
    return y.reshape(bsz * t, heads * dv)


def _rwkv_project(x_ref, g_ref, mu_ref, wrkv_ref, w0_ref, w1_ref, w2_ref, a0_ref, a1_ref, a2_ref,
                  g1_ref, g2_ref, r_ref, k_ref, v_ref, w_ref, a_ref, go_ref, prev_ref):
    hn = _rms(x_ref[0], g_ref[...], NORM_EPS)
    tm = hn.shape[0]
    ext = jnp.concatenate([prev_ref[...], hn], axis=0)
    xx = pltpu.roll(ext, 1, 0)[SUBLANES:, :] - hn
    prev_ref[...] = hn[tm - SUBLANES:, :]
    mu = mu_ref[...]

    def mix(i):
        return (hn + xx * mu[i:i + 1, :]).astype(BF16)

    r_ref[0] = jnp.dot(mix(0), wrkv_ref[0], preferred_element_type=F32)
    k_ref[0] = jnp.dot(mix(1), wrkv_ref[1], preferred_element_type=F32)
    v_ref[0] = jnp.dot(mix(2), wrkv_ref[2], preferred_element_type=F32)
    w_ref[0] = w0_ref[...] + _dot(jnp.tanh(jnp.dot(mix(3), w1_ref[...], preferred_element_type=F32)), w2_ref[...])
    a_ref[0] = a0_ref[...] + _dot(jnp.dot(mix(4), a1_ref[...], preferred_element_type=F32), a2_ref[...])
    go_ref[0] = _dot(_sigmoid(jnp.dot(mix(5), g1_ref[...], preferred_element_type=F32)), g2_ref[...])


def _rwkv_chunk_local(r, k, v, w_pre, a_pre, kk_w, ka_w, causal, strict, *, hd, chunk):
    tt, width = r.shape
    n_heads = width // hd
    lane = lax.broadcasted_iota(jnp.int32, (1, width), 1)
    head_of_lane = [(lane >= j * hd) & (lane < (j + 1) * hd) for j in range(n_heads)]

    def by_head(parts):
        out = parts[-1]
        for m, p in zip(head_of_lane[:-1], parts[:-1]):
            out = jnp.where(m, p, out)
        return out

    log_w = -jnp.exp(-_softplus(-w_pre) - 0.5)
    a = _sigmoid(a_pre)
    kk = k * kk_w
    kk = kk * lax.rsqrt(_head_sum(kk * kk, hd) + L2_EPS)
    k2 = k * (1.0 + (a - 1.0) * ka_w)
    cum = _seg_cumsum(log_w, 0, chunk)
    inv = jnp.exp(-cum)
    a_t = -kk * jnp.exp(cum - log_w)
    b_t = kk * a * inv
    k_t = k2 * inv
    r_t = r * jnp.exp(cum)

    bk = jnp.concatenate([b_t, k_t], axis=0)
    lows, rhs, y0_parts, p_rb = [], [], [], []
    for m in head_of_lane:
        ar = jnp.concatenate([jnp.where(m, a_t, 0.0), jnp.where(m, r_t, 0.0)], axis=0)
        g = _dot_nt(ar, bk)
        lows.append(jnp.where(strict, -g[:tt, :tt], 0.0))
        a_ak = jnp.where(strict, g[:tt, tt:], 0.0)
        p_rb.append(jnp.where(causal, g[tt:, :tt], 0.0))
        p_rk = jnp.where(causal, g[tt:, tt:], 0.0)
        rhs.append(jnp.concatenate([_dot(a_ak, v), a_t], axis=1))
        y0_parts.append(_dot(p_rk, v))
    return dict(lows=lows, rhs=rhs, y0=by_head(y0_parts), p_rb=p_rb, by_head=by_head,
                b_t=b_t, k_t=k_t, r_t=r_t, cum=cum, k2=k2, r=r, v=v)


def _head_sum(x, hd):
    lane = lax.broadcasted_iota(jnp.int32, (1, x.shape[1]), 1)
    tot = jnp.zeros_like(x)
    for j in range(x.shape[1] // hd):
        m = (lane >= j * hd) & (lane < (j + 1) * hd)
        tot = tot + jnp.where(m, jnp.sum(jnp.where(m, x, 0.0), axis=-1, keepdims=True), 0.0)
    return tot


def _rwkv_kernel(x_ref, gn_ref, mu_ref, wrkv_ref, w0_ref, w1_ref, w2_ref, a0_ref, a1_ref, a2_ref, g1_ref, g2_ref,
                 kk_ref, ka_ref, rk_ref, lnw_ref, lnb_ref, y_ref,
                 prev_ref, s_ref, r_ref, k_ref, v_ref, w_ref, a_ref, g_ref, *, hd, chunk):
    @pl.when(pl.program_id(1) == 0)
    def _init():
        prev_ref[...] = jnp.zeros_like(prev_ref)
        s_ref[...] = jnp.zeros_like(s_ref)

    _rwkv_project(x_ref, gn_ref, mu_ref, wrkv_ref, w0_ref, w1_ref, w2_ref, a0_ref, a1_ref, a2_ref, g1_ref, g2_ref,
                  r_ref, k_ref, v_ref, w_ref, a_ref, g_ref, prev_ref)
    _rwkv_recurrence(r_ref, k_ref, v_ref, w_ref, a_ref, g_ref, kk_ref, ka_ref, rk_ref, lnw_ref, lnb_ref,
                     y_ref, s_ref, hd=hd, chunk=chunk)


def _rwkv_recurrence(r_ref, k_ref, v_ref, w_ref, a_ref, g_ref, kk_ref, ka_ref, rk_ref, lnw_ref, lnb_ref,
                     y_ref, s_ref, *, hd, chunk):
    tt, width = r_ref.shape[1:]
    shift = chunk.bit_length() - 1
    rowi = lax.broadcasted_iota(jnp.int32, (tt, tt), 0)
    coli = lax.broadcasted_iota(jnp.int32, (tt, tt), 1)
    same_chunk = (rowi >> shift) == (coli >> shift)
    causal = same_chunk & (rowi >= coli)
    strict = same_chunk & (rowi > coli)
    srow = lax.broadcasted_iota(jnp.int32, (LANES, LANES), 0)
    scol = lax.broadcasted_iota(jnp.int32, (LANES, LANES), 1)
    same_head = None
    for j in range(LANES // hd):
        blk = (srow >= j * hd) & (srow < (j + 1) * hd) & (scol >= j * hd) & (scol < (j + 1) * hd)
        same_head = blk if same_head is None else (same_head | blk)

    groups = []
    for grp in range(width // LANES):
        ls = slice(grp * LANES, (grp + 1) * LANES)
        loc = _rwkv_chunk_local(r_ref[0, :, ls], k_ref[0, :, ls], v_ref[0, :, ls], w_ref[0, :, ls], a_ref[0, :, ls],
                                kk_ref[:, ls], ka_ref[:, ls], causal, strict, hd=hd, chunk=chunk)
        loc.update(state=s_ref[grp], us=[], rs=[])
        groups.append(loc)
    t_invs = _tri_inv([low for gp in groups for low in gp["lows"]], chunk)
    for gi, gp in enumerate(groups):
        n_h = len(gp["lows"])
        sols = [_dot(t_inv, rhs) for t_inv, rhs in zip(t_invs[gi * n_h:(gi + 1) * n_h], gp["rhs"])]
        gp.update(u0=gp["by_head"]([s[:, :LANES] for s in sols]), w_t=gp["by_head"]([s[:, LANES:] for s in sols]))

    for c in range(tt // chunk):
        sl = slice(c * chunk, (c + 1) * chunk)
        for gp in groups:
            u_c = gp["u0"][sl] + _dot_nt(gp["w_t"][sl], gp["state"])
            gp["rs"].append(_dot_nt(gp["r_t"][sl], gp["state"]))
            decay_end = jnp.exp(gp["cum"][(c + 1) * chunk - 1:(c + 1) * chunk, :])
            upd = gp["state"] + _dot_tn(u_c, gp["b_t"][sl]) + _dot_tn(gp["v"][sl], gp["k_t"][sl])
            gp["state"] = jnp.where(same_head, upd, 0.0) * decay_end
            gp["us"].append(u_c)

    for grp, gp in enumerate(groups):
        ls = slice(grp * LANES, (grp + 1) * LANES)
        s_ref[grp] = gp["state"]
        u = jnp.concatenate(gp["us"], axis=0)
        y = jnp.concatenate(gp["rs"], axis=0) + gp["y0"] + gp["by_head"]([_dot(p, u) for p in gp["p_rb"]])
        mean = _head_sum(y, hd) * (1.0 / hd)
        yc = y - mean
        var = _head_sum(yc * yc, hd) * (1.0 / hd)
        y = yc * lax.rsqrt(var + RWKV_GN_EPS) * lnw_ref[:, ls] + lnb_ref[:, ls]
        bonus = _head_sum(gp["r"] * gp["k2"] * rk_ref[:, ls], hd) * gp["v"]
        y_ref[0, :, ls] = ((y + bonus) * g_ref[0, :, ls]).astype(y_ref.dtype)


def _rwkv_mixer(x2d, g_in, shape, mu, w_rkv, w0, w1, w2, a0, a1, a2, g1, g2, k_k, k_a, r_k, ln_w, ln_b):
    bsz, t, d = shape
    hd = r_k.shape[1]
    tile = pl.BlockSpec((1, TIME_TILE, d), lambda b, i: (b, i, 0))
    vec = _resident((1, d))
    proj = pltpu.VMEM((1, TIME_TILE, d), F32)
    y = pl.pallas_call(
        functools.partial(_rwkv_kernel, hd=hd, chunk=RWKV_CHUNK),
        grid=(bsz, t // TIME_TILE),
        in_specs=[tile, vec, _resident(mu.shape), _resident(w_rkv.shape), vec, _resident(w1.shape),
                  _resident(w2.shape), vec, _resident(a1.shape), _resident(a2.shape), _resident(g1.shape),
                  _resident(g2.shape)] + [vec] * 5,
        out_specs=tile,
        out_shape=jax.ShapeDtypeStruct((bsz, t, d), BF16),
        scratch_shapes=[pltpu.VMEM((SUBLANES, d), F32), pltpu.VMEM((d // LANES, LANES, LANES), F32)] + [proj] * 6,
        compiler_params=_params(2, 56),
        name="rwkv",
    )(x2d.reshape(bsz, t, d), g_in.reshape(1, d), mu, w_rkv.astype(BF16), w0.reshape(1, d), w1.astype(BF16),
      w2.astype(BF16), a0.reshape(1, d), a1.astype(BF16), a2.astype(BF16), g1.astype(BF16), g2.astype(BF16),
      k_k.reshape(1, d), k_a.reshape(1, d), r_k.reshape(1, d), ln_w.reshape(1, d), ln_b.reshape(1, d))
    return y.reshape(bsz * t, d)


def _ssd_kernel(x_ref, gn_ref, w_ref, wdtt_ref, cw_ref, cb_ref, dtbc_ref, dtbr_ref, alc_ref, alr_ref, dskip_ref, nw_ref,
                y_ref, tail_ref, h_ref, *, groups, gh, hp, ns, chunk):
    @pl.when(pl.program_id(1) == 0)
    def _init():
        tail_ref[...] = jnp.zeros_like(tail_ref)
        h_ref[...] = jnp.zeros_like(h_ref)

    gw = gh * hp
    inner = groups * gw
    n_xbc = cw_ref.shape[1]
    xn = _rms(x_ref[0], gn_ref[...], NORM_EPS).astype(BF16)
    z = jnp.dot(xn, w_ref[:, :inner], preferred_element_type=F32)
    xbc = jnp.dot(xn, w_ref[:, inner:inner + n_xbc], preferred_element_type=F32)
    xbc = _silu(_causal_conv(xbc, tail_ref, cw_ref[...]) + cb_ref[...])
    dt_c = _softplus(jnp.dot(xn, w_ref[:, inner + n_xbc:], preferred_element_type=F32) + dtbc_ref[...])
    dt_r = _softplus(_dot_nt(wdtt_ref[...], xn) + dtbr_ref[...])
    acs_c = _seg_cumsum(dt_c * -jnp.exp(alc_ref[...]), 0, chunk)
    acs_r = _seg_cumsum(dt_r * -jnp.exp(alr_ref[...]), 1, chunk)
    tt = xbc.shape[0]

    n_heads = groups * gh
    head_row = lax.broadcasted_iota(jnp.int32, (n_heads, inner), 0)
    head_of_lane = lax.broadcasted_iota(jnp.int32, (n_heads, inner), 1) >> (hp.bit_length() - 1)
    spread = jnp.where(head_row == head_of_lane, 1.0, 0.0).astype(BF16)

    def expand(cols):
        rows = cols.shape[0]
        hi = cols.astype(BF16)
        rest = cols - hi.astype(F32)
        mid = rest.astype(BF16)
        lo = (rest - mid.astype(F32)).astype(BF16)
        parts = jnp.dot(jnp.concatenate([hi, mid, lo], axis=0), spread, preferred_element_type=F32)
        return parts[:rows] + parts[rows:2 * rows] + parts[2 * rows:]

    rowi = lax.broadcasted_iota(jnp.int32, (chunk, chunk), 0)
    coli = lax.broadcasted_iota(jnp.int32, (chunk, chunk), 1)
    causal = rowi >= coli
    pair_lo = lax.broadcasted_iota(jnp.int32, (1, 2 * hp), 1) < hp

    states = [h_ref[g] for g in range(groups)]
    ys = [[] for _ in range(groups)]
    for c in range(tt // chunk):
        sl = slice(c * chunk, (c + 1) * chunk)
        ac_c = acs_c[sl]
        ac_r = acs_r[:, sl]
        tot = ac_c[chunk - 1:chunk, :]
        dt_x = expand(dt_c[sl])
        in_decay_x = expand(jnp.exp(tot - ac_c))
        out_decay_x = expand(jnp.exp(ac_c))
        for g in range(groups):
            first = g * gh
            gl = slice(g * gw, (g + 1) * gw)
            xc = xbc[sl, gl]
            bc = xbc[sl, inner + g * ns:inner + (g + 1) * ns]
            cc = xbc[sl, inner + groups * ns + g * ns:inner + groups * ns + (g + 1) * ns]
            xdt = xc * dt_x[:, gl]
            cb = _dot_nt(cc, bc)
            y_pairs = []
            for p in range(gh // 2):
                xp = xdt[:, 2 * p * hp:2 * (p + 1) * hp]
                halves = []
                for e in (first + 2 * p, first + 2 * p + 1):
                    seg = jnp.where(causal, ac_c[:, e:e + 1] - ac_r[e:e + 1, :], -jnp.inf)
                    halves.append(_dot(cb * jnp.exp(seg), xp))
                y_pairs.append(jnp.where(pair_lo, halves[0], halves[1]))
            y_diag = jnp.concatenate(y_pairs, axis=1)
            y_off = _dot(cc, states[g]) * out_decay_x[:, gl]
            new = _dot_tn(bc, xdt * in_decay_x[:, gl])
            states[g] = states[g] * out_decay_x[chunk - 1:chunk, gl] + new
            ys[g].append(y_diag + y_off + xc * dskip_ref[:, gl])
    for g in range(groups):
        h_ref[g] = states[g]
        ls = slice(g * gw, (g + 1) * gw)
        y = jnp.concatenate(ys[g], axis=0) * _silu(z[:, ls])
        y_ref[0, :, ls] = _rms(y, nw_ref[:, ls], SSD_NORM_EPS).astype(y_ref.dtype)


def _ssd_mixer(x2d, g_in, shape, w_in, conv_w, conv_b, dt_bias, a_log, d_skip, norm_w):
    bsz, t, d = shape
    n_heads = a_log.shape[0]
    inner = norm_w.shape[0]
    hp = inner // n_heads
    n_xbc = conv_w.shape[1]
    groups = SSD_GROUPS
    ns = (n_xbc - inner) // (2 * groups)
    gh = n_heads // groups
    w = w_in.astype(BF16)
    w_dt_t = w[:, inner + n_xbc:].T
    tile = SSD_TIME_TILE
    y = pl.pallas_call(
        functools.partial(_ssd_kernel, groups=groups, gh=gh, hp=hp, ns=ns, chunk=SSD_CHUNK),
        grid=(bsz, t // tile),
        in_specs=[pl.BlockSpec((1, tile, d), lambda b, i: (b, i, 0)), _resident((1, d)), _resident(w.shape),
                  _resident(w_dt_t.shape), _resident(conv_w.shape), _resident((1, n_xbc)),
                  _resident((1, n_heads)), _resident((n_heads, 1)), _resident((1, n_heads)), _resident((n_heads, 1)),
                  _resident((1, inner)), _resident((1, inner))],
        out_specs=pl.BlockSpec((1, tile, inner), lambda b, i: (b, i, 0)),
        out_shape=jax.ShapeDtypeStruct((bsz, t, inner), BF16),
        scratch_shapes=[pltpu.VMEM((SUBLANES, n_xbc), F32), pltpu.VMEM((groups, ns, inner // groups), F32)],
        compiler_params=_params(2, 56),
        name="ssd",
    )(x2d.reshape(bsz, t, d), g_in.reshape(1, d), w, w_dt_t, conv_w, conv_b.reshape(1, n_xbc),
      dt_bias.reshape(1, n_heads), dt_bias.reshape(n_heads, 1), a_log.reshape(1, n_heads), a_log.reshape(n_heads, 1),
      jnp.repeat(d_skip, hp).reshape(1, inner), norm_w.reshape(1, inner))
    return y.reshape(bsz * t, inner)


def _lru_kernel(x_ref, gn_ref, w_ref, cw_ref, cb_ref, wg_ref, bg_ref, lam_ref, y_ref, tail_ref, h_ref, a_ref, b_ref):
    @pl.when(pl.program_id(1) == 0)
    def _init():
        tail_ref[...] = jnp.zeros_like(tail_ref)
        h_ref[...] = jnp.zeros_like(h_ref)

    width = lam_ref.shape[1]
    xn = _rms(x_ref[0], gn_ref[...], NORM_EPS).astype(BF16)
    gate = jnp.dot(xn, w_ref[:, :width], preferred_element_type=F32)
    u = _causal_conv(jnp.dot(xn, w_ref[:, width:], preferred_element_type=F32), tail_ref, cw_ref[...]) + cb_ref[...]
    tt = u.shape[0]
    blocks, bw, _ = wg_ref.shape
    groups = (tt // SUBLANES, SUBLANES, bw)
    pos = lax.broadcasted_iota(jnp.int32, groups, 1)
    for n in range(blocks):
        ls = slice(n * bw, (n + 1) * bw)
        un = u[:, ls]
        gates = _dot(un, wg_ref[n])
        r = _sigmoid(gates[:, :bw] + bg_ref[0:1, ls])
        i = _sigmoid(gates[:, bw:] + bg_ref[1:2, ls])
        log_a = -LRU_C * r * _softplus(-lam_ref[:, ls])
        a = jnp.exp(log_a)
        b = jnp.sqrt(1.0 - jnp.exp(2.0 * log_a)) * (i * un)
        a, b = a.reshape(groups), b.reshape(groups)
        s = 1
        while s < SUBLANES:
            keep = pos >= s
            b = jnp.where(keep, a * pltpu.roll(b, s, 1) + b, b)
            a = jnp.where(keep, a * pltpu.roll(a, s, 1), a)
            s *= 2
        a_ref[:, ls] = a.reshape(tt, bw)
        b_ref[:, ls] = b.reshape(tt, bw)

    def carry(g, h):
        rows = pl.ds(pl.multiple_of(g * SUBLANES, SUBLANES), SUBLANES)
        hg = b_ref[rows, :] + a_ref[rows, :] * h
        b_ref[rows, :] = hg
        return hg[SUBLANES - 1:SUBLANES, :]

    h_ref[...] = lax.fori_loop(0, tt // SUBLANES, carry, h_ref[...], unroll=8)
    y_ref[0] = (_gelu_tanh(gate) * b_ref[...]).astype(y_ref.dtype)


def _lru_mixer(x2d, g_in, shape, w_in, conv_w, conv_b, w_gates, b_gates, lam):
    bsz, t, d = shape
    width = lam.shape[0]
    tile = pl.BlockSpec((1, TIME_TILE, d), lambda b, i: (b, i, 0))
    y = pl.pallas_call(
        _lru_kernel,
        grid=(bsz, t // TIME_TILE),
        in_specs=[tile, _resident((1, d)), _resident(w_in.shape), _resident(conv_w.shape), _resident((1, width)),
                  _resident(w_gates.shape), _resident(b_gates.shape), _resident((1, width))],
        out_specs=pl.BlockSpec((1, TIME_TILE, width), lambda b, i: (b, i, 0)),
        out_shape=jax.ShapeDtypeStruct((bsz, t, width), BF16),
        scratch_shapes=[pltpu.VMEM((SUBLANES, width), F32), pltpu.VMEM((1, width), F32),
                        pltpu.VMEM((TIME_TILE, width), F32), pltpu.VMEM((TIME_TILE, width), F32)],
        compiler_params=_params(2, 48),
        name="lru",
    )(x2d.reshape(bsz, t, d), g_in.reshape(1, d), w_in.astype(BF16), conv_w, conv_b.reshape(1, width),
      w_gates.astype(BF16), b_gates, lam.reshape(1, width))
    return y.reshape(bsz * t, width)


def kernel(x, norm_g, ffn_w_in, ffn_w_out, gdn_w_in, gdn_conv_w, gdn_a_log, gdn_dt_bias, gdn_norm_w, gdn_w_out, rwkv_mu, rwkv_w_rkv, rwkv_w0, rwkv_w1, rwkv_w2, rwkv_a0, rwkv_a1, rwkv_a2, rwkv_g1, rwkv_g2, rwkv_k_k, rwkv_k_a, rwkv_r_k, rwkv_ln_w, rwkv_ln_b, rwkv_w_out, ssd_w_in, ssd_conv_w, ssd_conv_b, ssd_dt_bias, ssd_a_log, ssd_d, ssd_norm_w, ssd_w_out, lru_w_in, lru_conv_w, lru_conv_b, lru_w_gates, lru_b_gates, lru_lam, lru_w_out):
    shape = x.shape
    bsz, t, d = shape
    depth = norm_g.shape[0]
    n_mixers = 4
    x2d = x.reshape(bsz * t, d)
    for i in range(depth):
        m, j = i % n_mixers, i // n_mixers
        g = norm_g[i]
        x2d = _ffn(x2d, g[0], g[1], ffn_w_in, ffn_w_out, (i, 0))
        if m == 0:
            y = _gdn_mixer(x2d, g[2], shape, gdn_w_in[j], gdn_conv_w[j], gdn_a_log[j], gdn_dt_bias[j], gdn_norm_w[j])
            w_o = gdn_w_out[j]
        elif m == 1:
            y = _rwkv_mixer(x2d, g[2], shape, rwkv_mu[j], rwkv_w_rkv[j], rwkv_w0[j], rwkv_w1[j], rwkv_w2[j],
                            rwkv_a0[j], rwkv_a1[j], rwkv_a2[j], rwkv_g1[j], rwkv_g2[j], rwkv_k_k[j], rwkv_k_a[j],
                            rwkv_r_k[j], rwkv_ln_w[j], rwkv_ln_b[j])
            w_o = rwkv_w_out[j]
        elif m == 2:
            y = _ssd_mixer(x2d, g[2], shape, ssd_w_in[j], ssd_conv_w[j], ssd_conv_b[j], ssd_dt_bias[j],
                           ssd_a_log[j], ssd_d[j], ssd_norm_w[j])
            w_o = ssd_w_out[j]
        else:
            y = _lru_mixer(x2d, g[2], shape, lru_w_in[j], lru_conv_w[j], lru_conv_b[j], lru_w_gates[j],
                           lru_b_gates[j], lru_lam[j])
            w_o = lru_w_out[j]
        x2d = _mix_out_ffn(y, w_o.astype(BF16), g[3], x2d, g[4], g[5], ffn_w_in, ffn_w_out, (i, 1))
    return x2d.reshape(shape)
```

```python
import functools

import jax
import jax.numpy as jnp
from jax import lax
from jax.experimental import pallas as pl
from jax.experimental.pallas import tpu as pltpu

F32 = jnp.float32
BF16 = jnp.bfloat16

V7X_VMEM_BYTES = 64 * 1024 * 1024
SUBLANES = 8
LANES = 128
MXU_WIDTH = 256

NORM_EPS = 1e-6
L2_EPS = 1e-6
SSD_NORM_EPS = 1e-5
RWKV_GN_EPS = 64e-5
LRU_C = 8.0

ROW_TILE = 512
TIME_TILE = 256
GDN_CHUNK = 64
RWKV_CHUNK = 64
SSD_CHUNK = 128
SSD_TIME_TILE = 512
SSD_GROUPS = 4
CONV_TAPS = 4
WEIGHT_PIECES = 8


def _params(n_axes, vmem_mib):
    return pltpu.CompilerParams(
        dimension_semantics=("arbitrary",) * n_axes,
        vmem_limit_bytes=vmem_mib * 1024 * 1024,
    )


def _resident(shape):
    zeros = (0,) * len(shape)
    return pl.BlockSpec(shape, lambda *_: zeros, pipeline_mode=pl.Buffered(1))


def _rms(x, g, eps):
    return x * lax.rsqrt(jnp.mean(x * x, axis=-1, keepdims=True) + eps) * g


def _sigmoid(x):
    return 1.0 / (1.0 + jnp.exp(-x))


def _silu(x):
    return x * _sigmoid(x)


def _softplus(x):
    return jnp.maximum(x, 0.0) + jnp.log1p(jnp.exp(-jnp.abs(x)))


def _gelu_tanh(x):
    return 0.5 * x * (1.0 + jnp.tanh(0.7978845608028654 * (x + 0.044715 * (x * x * x))))


def _dot(a, b):
    return jnp.dot(a.astype(BF16), b.astype(BF16), preferred_element_type=F32)


def _dot_nt(a, b):
    return lax.dot_general(a.astype(BF16), b.astype(BF16), (((1,), (1,)), ((), ())),
                           preferred_element_type=F32)


def _dot_tn(a, b):
    return lax.dot_general(a.astype(BF16), b.astype(BF16), (((0,), (0,)), ((), ())),
                           preferred_element_type=F32)


def _seg_cumsum(x, axis, seg):
    pos = lax.broadcasted_iota(jnp.int32, x.shape, axis) & (seg - 1)
    s = 1
    while s < seg:
        x = x + jnp.where(pos >= s, pltpu.roll(x, s, axis), 0.0)
        s *= 2
    return x


def _causal_conv(pre, tail_ref, w):
    tt = pre.shape[0]
    ext = jnp.concatenate([tail_ref[...], pre], axis=0)
    y = pre * w[CONV_TAPS - 1:CONV_TAPS, :]
    for k in range(1, CONV_TAPS):
        y = y + pltpu.roll(ext, k, 0)[SUBLANES:, :] * w[CONV_TAPS - 1 - k:CONV_TAPS - k, :]
    tail_ref[...] = pre[tt - SUBLANES:, :]
    return y


def _tri_inv(lows, block):
    n = lows[0].shape[0]
    n_sys = n // block
    row = lax.broadcasted_iota(jnp.int32, (n, n), 0)
    col = lax.broadcasted_iota(jnp.int32, (n, n), 1)
    on_diag = (row >> (block.bit_length() - 1)) == (col >> (block.bit_length() - 1))

    def to_diag(m_l):
        return jnp.where(on_diag, jnp.concatenate([m_l] * n_sys, axis=0), 0.0)

    row_l = lax.broadcasted_iota(jnp.int32, (block, n), 0)
    col_l = lax.broadcasted_iota(jnp.int32, (block, n), 1) & (block - 1)
    eye_l = jnp.where(row_l == col_l, 1.0, 0.0)
    pair_l = (row_l >> 1) == (col_l >> 1)
    t_ls = []
    for low in lows:
        low_l = low[:block]
        for s in range(1, n_sys):
            low_l = low_l + low[s * block:(s + 1) * block]
        t_ls.append(eye_l - jnp.where(pair_l, low_l, 0.0))
    b, sh = 2, 1
    while b < block:
        sub = ((row >> (sh + 1)) == (col >> (sh + 1))) & (((row >> sh) & 1) == 1) & (((col >> sh) & 1) == 0)
        xs = [_dot(t_l, jnp.where(sub, low, 0.0)) for t_l, low in zip(t_ls, lows)]
        t_ls = [t_l - _dot(x, to_diag(t_l)) for t_l, x in zip(t_ls, xs)]
        b, sh = 2 * b, sh + 1
    return [to_diag(t_l) for t_l in t_ls]


def _ffn_half_step(x, g0, g1, win_ref, wout_ref):
    d_ff = wout_ref.shape[0]
    mid = (d_ff // MXU_WIDTH // 2) * MXU_WIDTH
    xn = _rms(x, g0, NORM_EPS).astype(BF16)
    acc = None
    for lo, hi in ((0, mid), (mid, d_ff)) if 0 < mid < d_ff else ((0, d_ff),):
        gate = jnp.dot(xn, win_ref[:, lo:hi], preferred_element_type=F32)
        up = jnp.dot(xn, win_ref[:, d_ff + lo:d_ff + hi], preferred_element_type=F32)
        act = (_silu(gate) * up).astype(BF16)
        part = jnp.dot(act, wout_ref[lo:hi, :], preferred_element_type=F32)
        acc = part if acc is None else acc + part
    return x + _rms(acc, 0.5 * g1, NORM_EPS)


def _stage_weight(src_hbm, idx, dst_ref, stage_ref, sem_ref):
    rows = stage_ref.shape[1]

    def copy(c):
        piece = src_hbm.at[idx[0], idx[1], pl.ds(c * rows, rows)]
        return pltpu.make_async_copy(piece, stage_ref.at[c % 2], sem_ref.at[c % 2])

    n_pieces = dst_ref.shape[0] // rows
    copy(0).start()
    for c in range(n_pieces):
        if c + 1 < n_pieces:
            copy(c + 1).start()
        copy(c).wait()
        dst_ref[c * rows:(c + 1) * rows, :] = stage_ref[c % 2].astype(BF16)


def _ffn_weight_scratch(w_in, w_out):
    (d, f2), (f, _) = w_in.shape[2:], w_out.shape[2:]
    return [pltpu.VMEM((d, f2), BF16), pltpu.VMEM((f, d), BF16),
            pltpu.VMEM((2, d // WEIGHT_PIECES, f2), F32), pltpu.VMEM((2, f // WEIGHT_PIECES, d), F32),
            pltpu.SemaphoreType.DMA((2,)), pltpu.SemaphoreType.DMA((2,))]


def _load_ffn_weights(win_hbm, wout_hbm, idx, win_ref, wout_ref, sin_ref, sout_ref, semi_ref, semo_ref):
    @pl.when(pl.program_id(0) == 0)
    def _load():
        _stage_weight(win_hbm, idx, win_ref, sin_ref, semi_ref)
        _stage_weight(wout_hbm, idx, wout_ref, sout_ref, semo_ref)


def _ffn_kernel(x_ref, g0_ref, g1_ref, win_hbm, wout_hbm, o_ref, win_ref, wout_ref, *stage, idx):
    _load_ffn_weights(win_hbm, wout_hbm, idx, win_ref, wout_ref, *stage)
    o_ref[...] = _ffn_half_step(x_ref[...], g0_ref[...], g1_ref[...], win_ref, wout_ref)


def _ffn(x2d, g0, g1, w_in, w_out, idx):
    n, d = x2d.shape
    row = pl.BlockSpec((ROW_TILE, d), lambda i: (i, 0))
    hbm = pl.BlockSpec(memory_space=pl.ANY)
    return pl.pallas_call(
        functools.partial(_ffn_kernel, idx=idx),
        grid=(n // ROW_TILE,),
        in_specs=[row, _resident((1, d)), _resident((1, d)), hbm, hbm],
        out_specs=row,
        out_shape=jax.ShapeDtypeStruct((n, d), F32),
        scratch_shapes=_ffn_weight_scratch(w_in, w_out),
        compiler_params=_params(1, 56),
        name="ffn",
    )(x2d, g0.reshape(1, d), g1.reshape(1, d), w_in, w_out)


def _mix_out_ffn_kernel(y_ref, wo_ref, go_ref, x_ref, g0_ref, g1_ref, win_hbm, wout_hbm, o_ref, win_ref, wout_ref,
                        *stage, idx):
    _load_ffn_weights(win_hbm, wout_hbm, idx, win_ref, wout_ref, *stage)
    h = jnp.dot(y_ref[...], wo_ref[...], preferred_element_type=F32)
    x = x_ref[...] + _rms(h, go_ref[...], NORM_EPS)
    o_ref[...] = _ffn_half_step(x, g0_ref[...], g1_ref[...], win_ref, wout_ref)


def _mix_out_ffn(y2d, w_o, g_o, x2d, g0, g1, w_in, w_out, idx):
    n, d = x2d.shape
    k = y2d.shape[1]
    row = pl.BlockSpec((ROW_TILE, d), lambda i: (i, 0))
    vec = _resident((1, d))
    hbm = pl.BlockSpec(memory_space=pl.ANY)
    return pl.pallas_call(
        functools.partial(_mix_out_ffn_kernel, idx=idx),
        grid=(n // ROW_TILE,),
        in_specs=[pl.BlockSpec((ROW_TILE, k), lambda i: (i, 0)), _resident(w_o.shape), vec, row, vec, vec, hbm, hbm],
        out_specs=row,
        out_shape=jax.ShapeDtypeStruct((n, d), F32),
        scratch_shapes=_ffn_weight_scratch(w_in, w_out),
        compiler_params=_params(1, 56),
        name="mix_out_ffn",
    )(y2d, w_o, g_o.reshape(1, d), x2d, g0.reshape(1, d), g1.reshape(1, d), w_in, w_out)


def _gdn_chunk_local(q, k, v, beta, gc_col, gc_row, causal, strict):
    decay = jnp.exp(jnp.where(causal, gc_col - gc_row, -jnp.inf))
    lower = jnp.where(strict, _dot_nt(k, k) * decay * beta, 0.0)
    eg = jnp.exp(gc_col)
    rhs = jnp.concatenate([v * beta, k * (beta * eg)], axis=1)
    return lower, rhs, _dot_nt(q, k) * decay, q * eg


def _gdn_kernel(x_ref, gn_ref, w_ref, wbat_ref, cw_ref, alc_ref, dtc_ref, alr_ref, dtr_ref, nw_ref, y_ref,
                tail_ref, s_ref, *, heads, dk, dv, chunk):
    @pl.when(pl.program_id(1) == 0)
    def _init():
        tail_ref[...] = jnp.zeros_like(tail_ref)
        s_ref[...] = jnp.zeros_like(s_ref)

    n_qkv = cw_ref.shape[1]
    o_k, o_v, o_ba = heads * dk, 2 * heads * dk, n_qkv + heads * dv
    xn = _rms(x_ref[0], gn_ref[...], NORM_EPS).astype(BF16)
    qkv = _silu(_causal_conv(jnp.dot(xn, w_ref[:, :n_qkv], preferred_element_type=F32), tail_ref, cw_ref[...]))
    z = jnp.dot(xn, w_ref[:, n_qkv:o_ba], preferred_element_type=F32)
    ba_c = jnp.dot(xn, w_ref[:, o_ba:], preferred_element_type=F32)
    ba_r = _dot_nt(wbat_ref[...], xn)
    beta_all = _sigmoid(ba_c)
    gc_cols = _seg_cumsum(-jnp.exp(alc_ref[...]) * _softplus(ba_c + dtc_ref[...]), 0, chunk)
    gc_rows = _seg_cumsum(-jnp.exp(alr_ref[...]) * _softplus(ba_r[heads:, :] + dtr_ref[...]), 1, chunk)
    tt = qkv.shape[0]
    shift = chunk.bit_length() - 1
    row = lax.broadcasted_iota(jnp.int32, (tt, tt), 0)
    col = lax.broadcasted_iota(jnp.int32, (tt, tt), 1)
    same_chunk = (row >> shift) == (col >> shift)
    causal = same_chunk & (row >= col)
    strict = same_chunk & (row > col)
    hds = []
    for h in range(heads):
        qh = qkv[:, h * dk:(h + 1) * dk]
        kh = qkv[:, o_k + h * dk:o_k + (h + 1) * dk]
        vh = qkv[:, o_v + h * dv:o_v + (h + 1) * dv]
        qh = qh * (lax.rsqrt(jnp.sum(qh * qh, axis=-1, keepdims=True) + L2_EPS) * dk ** -0.5)
        kh = kh * lax.rsqrt(jnp.sum(kh * kh, axis=-1, keepdims=True) + L2_EPS)
        beta = beta_all[:, h:h + 1]
        gc_col = gc_cols[:, heads + h:heads + h + 1]
        gc_row = gc_rows[h:h + 1, :]
        lower, rhs, qk, q_dec = _gdn_chunk_local(qh, kh, vh, beta, gc_col, gc_row, causal, strict)
        hds.append(dict(lower=lower, rhs=rhs, qk=qk, q_dec=q_dec, k=kh, gc=gc_col, state=s_ref[h],
                        v_new=[], o_state=[]))
    for hd, t_inv in zip(hds, _tri_inv([hd["lower"] for hd in hds], chunk)):
        sol = _dot(t_inv, hd["rhs"])
        hd.update(u=sol[:, :dv], w=sol[:, dv:])

    for c in range(tt // chunk):
        sl = slice(c * chunk, (c + 1) * chunk)
        for hd in hds:
            g_last = hd["gc"][(c + 1) * chunk - 1:(c + 1) * chunk, :]
            vn = hd["u"][sl] - _dot(hd["w"][sl], hd["state"])
            hd["o_state"].append(_dot(hd["q_dec"][sl], hd["state"]))
            hd["state"] = hd["state"] * jnp.exp(g_last) + _dot_tn(hd["k"][sl] * jnp.exp(g_last - hd["gc"][sl]), vn)
            hd["v_new"].append(vn)

    for h, hd in enumerate(hds):
        s_ref[h] = hd["state"]
        o = jnp.concatenate(hd["o_state"], axis=0) + _dot(hd["qk"], jnp.concatenate(hd["v_new"], axis=0))
        o = _rms(o, nw_ref[...], NORM_EPS) * _silu(z[:, h * dv:(h + 1) * dv])
        y_ref[0, :, h * dv:(h + 1) * dv] = o.astype(y_ref.dtype)


def _gdn_mixer(x2d, g_in, shape, w_in, conv_w, a_log, dt_bias, norm_w):
    bsz, t, d = shape
    heads = a_log.shape[0]
    dv = norm_w.shape[0]
    n_qkv = conv_w.shape[1]
    dk = (n_qkv - heads * dv) // (2 * heads)
    w = w_in.astype(BF16)
    w_ba_t = w[:, n_qkv + heads * dv:].T
    unused = jnp.zeros_like(a_log)
    tile = pl.BlockSpec((1, TIME_TILE, d), lambda b, i: (b, i, 0))
    y = pl.pallas_call(
        functools.partial(_gdn_kernel, heads=heads, dk=dk, dv=dv, chunk=GDN_CHUNK),
        grid=(bsz, t // TIME_TILE),
        in_specs=[tile, _resident((1, d)), _resident(w.shape), _resident(w_ba_t.shape), _resident(conv_w.shape),
                  _resident((1, 2 * heads)), _resident((1, 2 * heads)), _resident((heads, 1)), _resident((heads, 1)),
                  _resident((1, dv))],
        out_specs=pl.BlockSpec((1, TIME_TILE, heads * dv), lambda b, i: (b, i, 0)),
        out_shape=jax.ShapeDtypeStruct((bsz, t, heads * dv), BF16),
        scratch_shapes=[pltpu.VMEM((SUBLANES, n_qkv), F32), pltpu.VMEM((heads, dk, dv), F32)],
        compiler_params=_params(2, 56),
        name="gdn",
    )(x2d.reshape(bsz, t, d), g_in.reshape(1, d), w, w_ba_t, conv_w,
      jnp.concatenate([unused, a_log]).reshape(1, 2 * heads), jnp.concatenate([unused, dt_bias]).reshape(1, 2 * heads),
      a_log.reshape(heads, 1), dt_bias.reshape(heads, 1), norm_w.reshape(1, dv))
    return y.reshape(bsz * t, heads * dv)


def _rwkv_project(x_ref, g_ref, mu_ref, wrkv_ref, w0_ref, w1_ref, w2_ref, a0_ref, a1_ref, a2_ref,
                  g1_ref, g2_ref, r_ref, k_ref, v_ref, w_ref, a_ref, go_ref, prev_ref):
    hn = _rms(x_ref[0], g_ref[...], NORM_EPS)
    tm = hn.shape[0]
    ext = jnp.concatenate([prev_ref[...], hn], axis=0)
    xx = pltpu.roll(ext, 1, 0)[SUBLANES:, :] - hn
    prev_ref[...] = hn[tm - SUBLANES:, :]
    mu = mu_ref[...]

    def mix(i):
        return (hn + xx * mu[i:i + 1, :]).astype(BF16)

    r_ref[0] = jnp.dot(mix(0), wrkv_ref[0], preferred_element_type=F32)
    k_ref[0] = jnp.dot(mix(1), wrkv_ref[1], preferred_element_type=F32)
    v_ref[0] = jnp.dot(mix(2), wrkv_ref[2], preferred_element_type=F32)
    w_ref[0] = w0_ref[...] + _dot(jnp.tanh(jnp.dot(mix(3), w1_ref[...], preferred_element_type=F32)), w2_ref[...])
    a_ref[0] = a0_ref[...] + _dot(jnp.dot(mix(4), a1_ref[...], preferred_element_type=F32), a2_ref[...])
    go_ref[0] = _dot(_sigmoid(jnp.dot(mix(5), g1_ref[...], preferred_element_type=F32)), g2_ref[...])


def _rwkv_chunk_local(r, k, v, w_pre, a_pre, kk_w, ka_w, *, hd, chunk):
    tt, width = r.shape
    n_heads = width // hd
    lane = lax.broadcasted_iota(jnp.int32, (1, width), 1)
    head_of_lane = [(lane >= j * hd) & (lane < (j + 1) * hd) for j in range(n_heads)]

    def by_head(parts):
        out = parts[-1]
        for m, p in zip(head_of_lane[:-1], parts[:-1]):
            out = jnp.where(m, p, out)
        return out

    log_w = -jnp.exp(-_softplus(-w_pre) - 0.5)
    a = _sigmoid(a_pre)
    kk = k * kk_w
    kk = kk * lax.rsqrt(_head_sum(kk * kk, hd) + L2_EPS)
    k2 = k * (1.0 + (a - 1.0) * ka_w)
    cum = _seg_cumsum(log_w, 0, chunk)
    inv = jnp.exp(-cum)
    a_t = -kk * jnp.exp(cum - log_w)
    b_t = kk * a * inv
    k_t = k2 * inv
    r_t = r * jnp.exp(cum)

    n_chunks = tt // chunk
    lane_l = lax.broadcasted_iota(jnp.int32, (chunk, LANES), 1)
    row_l = lax.broadcasted_iota(jnp.int32, (chunk, LANES), 0)
    col_l = lane_l & (chunk - 1)
    upper_half = lane_l >= chunk
    zeros = jnp.zeros((chunk, LANES), F32)
    v_swapped = jnp.concatenate([v[(c ^ 1) * chunk:((c ^ 1) + 1) * chunk] for c in range(n_chunks)], axis=0)

    def place(block, c):
        return jnp.concatenate([block if col == c // 2 else zeros for col in range(n_chunks // 2)], axis=1)

    lows, rhs, y0_parts, p_rb = [], [], [], []
    for m in head_of_lane:
        a_m, r_m = jnp.where(m, a_t, 0.0), jnp.where(m, r_t, 0.0)
        ab, ak, rb, rk = [], [], [], []
        for c in range(n_chunks):
            sl = slice(c * chunk, (c + 1) * chunk)
            right = [b_t[sl], k_t[sl]] if c % 2 == 0 else [k_t[sl], b_t[sl]]
            g = _dot_nt(jnp.concatenate([a_m[sl], r_m[sl]], axis=0), jnp.concatenate(right, axis=0))
            b_half = upper_half if c % 2 else ~upper_half
            top, bot = g[:chunk], g[chunk:]
            ab.append(place(jnp.where(b_half & (row_l > col_l), -top, 0.0), c))
            ak.append(place(jnp.where(~b_half & (row_l > col_l), top, 0.0), c))
            rb.append(place(jnp.where(b_half & (row_l >= col_l), bot, 0.0), c))
            rk.append(place(jnp.where(~b_half & (row_l >= col_l), bot, 0.0), c))
        lows.append(jnp.concatenate(ab, axis=0))
        p_rb.append(jnp.concatenate(rb, axis=0))
        rhs.append(jnp.concatenate([_dot(jnp.concatenate(ak, axis=0), v_swapped), a_t], axis=1))
        y0_parts.append(_dot(jnp.concatenate(rk, axis=0), v_swapped))
    return dict(lows=lows, rhs=rhs, y0=by_head(y0_parts), p_rb=p_rb, by_head=by_head,
                b_t=b_t, k_t=k_t, r_t=r_t, cum=cum, k2=k2, r=r, v=v)


def _head_sum(x, hd):
    lane = lax.broadcasted_iota(jnp.int32, (1, x.shape[1]), 1)
    tot = jnp.zeros_like(x)
    for j in range(x.shape[1] // hd):
        m = (lane >= j * hd) & (lane < (j + 1) * hd)
        tot = tot + jnp.where(m, jnp.sum(jnp.where(m, x, 0.0), axis=-1, keepdims=True), 0.0)
    return tot


def _rwkv_kernel(x_ref, gn_ref, mu_ref, wrkv_ref, w0_ref, w1_ref, w2_ref, a0_ref, a1_ref, a2_ref, g1_ref, g2_ref,
                 kk_ref, ka_ref, rk_ref, lnw_ref, lnb_ref, y_ref,
                 prev_ref, s_ref, r_ref, k_ref, v_ref, w_ref, a_ref, g_ref, *, hd, chunk):
    @pl.when(pl.program_id(1) == 0)
    def _init():
        prev_ref[...] = jnp.zeros_like(prev_ref)
        s_ref[...] = jnp.zeros_like(s_ref)

    _rwkv_project(x_ref, gn_ref, mu_ref, wrkv_ref, w0_ref, w1_ref, w2_ref, a0_ref, a1_ref, a2_ref, g1_ref, g2_ref,
                  r_ref, k_ref, v_ref, w_ref, a_ref, g_ref, prev_ref)
    _rwkv_recurrence(r_ref, k_ref, v_ref, w_ref, a_ref, g_ref, kk_ref, ka_ref, rk_ref, lnw_ref, lnb_ref,
                     y_ref, s_ref, hd=hd, chunk=chunk)


def _rwkv_recurrence(r_ref, k_ref, v_ref, w_ref, a_ref, g_ref, kk_ref, ka_ref, rk_ref, lnw_ref, lnb_ref,
                     y_ref, s_ref, *, hd, chunk):
    tt, width = r_ref.shape[1:]
    srow = lax.broadcasted_iota(jnp.int32, (LANES, LANES), 0)
    scol = lax.broadcasted_iota(jnp.int32, (LANES, LANES), 1)
    same_head = None
    for j in range(LANES // hd):
        blk = (srow >= j * hd) & (srow < (j + 1) * hd) & (scol >= j * hd) & (scol < (j + 1) * hd)
        same_head = blk if same_head is None else (same_head | blk)

    groups = []
    for grp in range(width // LANES):
        ls = slice(grp * LANES, (grp + 1) * LANES)
        loc = _rwkv_chunk_local(r_ref[0, :, ls], k_ref[0, :, ls], v_ref[0, :, ls], w_ref[0, :, ls], a_ref[0, :, ls],
                                kk_ref[:, ls], ka_ref[:, ls], hd=hd, chunk=chunk)
        loc.update(state=s_ref[grp], us=[], rs=[])
        groups.append(loc)
    t_invs = _tri_inv([low for gp in groups for low in gp["lows"]], chunk)
    for gi, gp in enumerate(groups):
        n_h = len(gp["lows"])
        sols = [_dot(t_inv, rhs) for t_inv, rhs in zip(t_invs[gi * n_h:(gi + 1) * n_h], gp["rhs"])]
        gp.update(u0=gp["by_head"]([s[:, :LANES] for s in sols]), w_t=gp["by_head"]([s[:, LANES:] for s in sols]))

    for c in range(tt // chunk):
        sl = slice(c * chunk, (c + 1) * chunk)
        for gp in groups:
            u_c = gp["u0"][sl] + _dot_nt(gp["w_t"][sl], gp["state"])
            gp["rs"].append(_dot_nt(gp["r_t"][sl], gp["state"]))
            decay_end = jnp.exp(gp["cum"][(c + 1) * chunk - 1:(c + 1) * chunk, :])
            upd = gp["state"] + _dot_tn(u_c, gp["b_t"][sl]) + _dot_tn(gp["v"][sl], gp["k_t"][sl])
            gp["state"] = jnp.where(same_head, upd, 0.0) * decay_end
            gp["us"].append(u_c)

    for grp, gp in enumerate(groups):
        ls = slice(grp * LANES, (grp + 1) * LANES)
        s_ref[grp] = gp["state"]
        u = jnp.concatenate(gp["us"], axis=0)
        y = jnp.concatenate(gp["rs"], axis=0) + gp["y0"] + gp["by_head"]([_dot(p, u) for p in gp["p_rb"]])
        mean = _head_sum(y, hd) * (1.0 / hd)
        yc = y - mean
        var = _head_sum(yc * yc, hd) * (1.0 / hd)
        y = yc * lax.rsqrt(var + RWKV_GN_EPS) * lnw_ref[:, ls] + lnb_ref[:, ls]
        bonus = _head_sum(gp["r"] * gp["k2"] * rk_ref[:, ls], hd) * gp["v"]
        y_ref[0, :, ls] = ((y + bonus) * g_ref[0, :, ls]).astype(y_ref.dtype)


def _rwkv_mixer(x2d, g_in, shape, mu, w_rkv, w0, w1, w2, a0, a1, a2, g1, g2, k_k, k_a, r_k, ln_w, ln_b):
    bsz, t, d = shape
    hd = r_k.shape[1]
    tile = pl.BlockSpec((1, TIME_TILE, d), lambda b, i: (b, i, 0))
    vec = _resident((1, d))
    proj = pltpu.VMEM((1, TIME_TILE, d), F32)
    y = pl.pallas_call(
        functools.partial(_rwkv_kernel, hd=hd, chunk=RWKV_CHUNK),
        grid=(bsz, t // TIME_TILE),
        in_specs=[tile, vec, _resident(mu.shape), _resident(w_rkv.shape), vec, _resident(w1.shape),
                  _resident(w2.shape), vec, _resident(a1.shape), _resident(a2.shape), _resident(g1.shape),
                  _resident(g2.shape)] + [vec] * 5,
        out_specs=tile,
        out_shape=jax.ShapeDtypeStruct((bsz, t, d), BF16),
        scratch_shapes=[pltpu.VMEM((SUBLANES, d), F32), pltpu.VMEM((d // LANES, LANES, LANES), F32)] + [proj] * 6,
        compiler_params=_params(2, 56),
        name="rwkv",
    )(x2d.reshape(bsz, t, d), g_in.reshape(1, d), mu, w_rkv.astype(BF16), w0.reshape(1, d), w1.astype(BF16),
      w2.astype(BF16), a0.reshape(1, d), a1.astype(BF16), a2.astype(BF16), g1.astype(BF16), g2.astype(BF16),
      k_k.reshape(1, d), k_a.reshape(1, d), r_k.reshape(1, d), ln_w.reshape(1, d), ln_b.reshape(1, d))
    return y.reshape(bsz * t, d)


def _ssd_kernel(x_ref, gn_ref, w_ref, wdtt_ref, cw_ref, cb_ref, dtbc_ref, dtbr_ref, alc_ref, alr_ref, dskip_ref, nw_ref,
                y_ref, tail_ref, h_ref, *, groups, gh, hp, ns, chunk):
    @pl.when(pl.program_id(1) == 0)
    def _init():
        tail_ref[...] = jnp.zeros_like(tail_ref)
        h_ref[...] = jnp.zeros_like(h_ref)

    gw = gh * hp
    inner = groups * gw
    n_xbc = cw_ref.shape[1]
    xn = _rms(x_ref[0], gn_ref[...], NORM_EPS).astype(BF16)
    z = jnp.dot(xn, w_ref[:, :inner], preferred_element_type=F32)
    xbc = jnp.dot(xn, w_ref[:, inner:inner + n_xbc], preferred_element_type=F32)
    xbc = _silu(_causal_conv(xbc, tail_ref, cw_ref[...]) + cb_ref[...])
    dt_c = _softplus(jnp.dot(xn, w_ref[:, inner + n_xbc:], preferred_element_type=F32) + dtbc_ref[...])
    dt_r = _softplus(_dot_nt(wdtt_ref[...], xn) + dtbr_ref[...])
    acs_c = _seg_cumsum(dt_c * -jnp.exp(alc_ref[...]), 0, chunk)
    acs_r = _seg_cumsum(dt_r * -jnp.exp(alr_ref[...]), 1, chunk)
    tt = xbc.shape[0]

    n_heads = groups * gh
    head_row = lax.broadcasted_iota(jnp.int32, (n_heads, inner), 0)
    head_of_lane = lax.broadcasted_iota(jnp.int32, (n_heads, inner), 1) >> (hp.bit_length() - 1)
    spread = jnp.where(head_row == head_of_lane, 1.0, 0.0).astype(BF16)

    def expand(cols):
        rows = cols.shape[0]
        hi = cols.astype(BF16)
        rest = cols - hi.astype(F32)
        mid = rest.astype(BF16)
        lo = (rest - mid.astype(F32)).astype(BF16)
        parts = jnp.dot(jnp.concatenate([hi, mid, lo], axis=0), spread, preferred_element_type=F32)
        return parts[:rows] + parts[rows:2 * rows] + parts[2 * rows:]

    rowi = lax.broadcasted_iota(jnp.int32, (chunk, chunk), 0)
    coli = lax.broadcasted_iota(jnp.int32, (chunk, chunk), 1)
    causal = rowi >= coli
    pair_lo = lax.broadcasted_iota(jnp.int32, (1, 2 * hp), 1) < hp

    states = [h_ref[g] for g in range(groups)]
    ys = [[] for _ in range(groups)]
    for c in range(tt // chunk):
        sl = slice(c * chunk, (c + 1) * chunk)
        ac_c = acs_c[sl]
        ac_r = acs_r[:, sl]
        tot = ac_c[chunk - 1:chunk, :]
        dt_x = expand(dt_c[sl])
        in_decay_x = expand(jnp.exp(tot - ac_c))
        out_decay_x = expand(jnp.exp(ac_c))
        for g in range(groups):
            first = g * gh
            gl = slice(g * gw, (g + 1) * gw)
            xc = xbc[sl, gl]
            bc = xbc[sl, inner + g * ns:inner + (g + 1) * ns]
            cc = xbc[sl, inner + groups * ns + g * ns:inner + groups * ns + (g + 1) * ns]
            xdt = xc * dt_x[:, gl]
            cb = _dot_nt(cc, bc)
            y_pairs = []
            for p in range(gh // 2):
                xp = xdt[:, 2 * p * hp:2 * (p + 1) * hp]
                halves = []
                for e in (first + 2 * p, first + 2 * p + 1):
                    seg = jnp.where(causal, ac_c[:, e:e + 1] - ac_r[e:e + 1, :], -jnp.inf)
                    halves.append(_dot(cb * jnp.exp(seg), xp))
                y_pairs.append(jnp.where(pair_lo, halves[0], halves[1]))
            y_diag = jnp.concatenate(y_pairs, axis=1)
            y_off = _dot(cc, states[g]) * out_decay_x[:, gl]
            new = _dot_tn(bc, xdt * in_decay_x[:, gl])
            states[g] = states[g] * out_decay_x[chunk - 1:chunk, gl] + new
            ys[g].append(y_diag + y_off + xc * dskip_ref[:, gl])
    for g in range(groups):
        h_ref[g] = states[g]
        ls = slice(g * gw, (g + 1) * gw)
        y = jnp.concatenate(ys[g], axis=0) * _silu(z[:, ls])
        y_ref[0, :, ls] = _rms(y, nw_ref[:, ls], SSD_NORM_EPS).astype(y_ref.dtype)


def _ssd_mixer(x2d, g_in, shape, w_in, conv_w, conv_b, dt_bias, a_log, d_skip, norm_w):
    bsz, t, d = shape
    n_heads = a_log.shape[0]
    inner = norm_w.shape[0]
    hp = inner // n_heads
    n_xbc = conv_w.shape[1]
    groups = SSD_GROUPS
    ns = (n_xbc - inner) // (2 * groups)
    gh = n_heads // groups
    w = w_in.astype(BF16)
    w_dt_t = w[:, inner + n_xbc:].T
    tile = SSD_TIME_TILE
    y = pl.pallas_call(
        functools.partial(_ssd_kernel, groups=groups, gh=gh, hp=hp, ns=ns, chunk=SSD_CHUNK),
        grid=(bsz, t // tile),
        in_specs=[pl.BlockSpec((1, tile, d), lambda b, i: (b, i, 0)), _resident((1, d)), _resident(w.shape),
                  _resident(w_dt_t.shape), _resident(conv_w.shape), _resident((1, n_xbc)),
                  _resident((1, n_heads)), _resident((n_heads, 1)), _resident((1, n_heads)), _resident((n_heads, 1)),
                  _resident((1, inner)), _resident((1, inner))],
        out_specs=pl.BlockSpec((1, tile, inner), lambda b, i: (b, i, 0)),
        out_shape=jax.ShapeDtypeStruct((bsz, t, inner), BF16),
        scratch_shapes=[pltpu.VMEM((SUBLANES, n_xbc), F32), pltpu.VMEM((groups, ns, inner // groups), F32)],
        compiler_params=_params(2, 56),
        name="ssd",
    )(x2d.reshape(bsz, t, d), g_in.reshape(1, d), w, w_dt_t, conv_w, conv_b.reshape(1, n_xbc),
      dt_bias.reshape(1, n_heads), dt_bias.reshape(n_heads, 1), a_log.reshape(1, n_heads), a_log.reshape(n_heads, 1),
      jnp.repeat(d_skip, hp).reshape(1, inner), norm_w.reshape(1, inner))
    return y.reshape(bsz * t, inner)


def _lru_kernel(x_ref, gn_ref, w_ref, cw_ref, cb_ref, wg_ref, bg_ref, lam_ref, y_ref, tail_ref, h_ref, a_ref, b_ref):
    @pl.when(pl.program_id(1) == 0)
    def _init():
        tail_ref[...] = jnp.zeros_like(tail_ref)
        h_ref[...] = jnp.zeros_like(h_ref)

    width = lam_ref.shape[1]
    xn = _rms(x_ref[0], gn_ref[...], NORM_EPS).astype(BF16)
    gate = jnp.dot(xn, w_ref[:, :width], preferred_element_type=F32)
    u = _causal_conv(jnp.dot(xn, w_ref[:, width:], preferred_element_type=F32), tail_ref, cw_ref[...]) + cb_ref[...]
    tt = u.shape[0]
    blocks, bw, _ = wg_ref.shape
    groups = (tt // SUBLANES, SUBLANES, bw)
    pos = lax.broadcasted_iota(jnp.int32, groups, 1)
    for n in range(blocks):
        ls = slice(n * bw, (n + 1) * bw)
        un = u[:, ls]
        gates = _dot(un, wg_ref[n])
        r = _sigmoid(gates[:, :bw] + bg_ref[0:1, ls])
        i = _sigmoid(gates[:, bw:] + bg_ref[1:2, ls])
        log_a = -LRU_C * r * _softplus(-lam_ref[:, ls])
        a = jnp.exp(log_a)
        b = jnp.sqrt(1.0 - jnp.exp(2.0 * log_a)) * (i * un)
        a, b = a.reshape(groups), b.reshape(groups)
        s = 1
        while s < SUBLANES:
            keep = pos >= s
            b = jnp.where(keep, a * pltpu.roll(b, s, 1) + b, b)
            a = jnp.where(keep, a * pltpu.roll(a, s, 1), a)
            s *= 2
        a_ref[:, ls] = a.reshape(tt, bw)
        b_ref[:, ls] = b.reshape(tt, bw)

    def carry(g, h):
        rows = pl.ds(pl.multiple_of(g * SUBLANES, SUBLANES), SUBLANES)
        hg = b_ref[rows, :] + a_ref[rows, :] * h
        b_ref[rows, :] = hg
        return hg[SUBLANES - 1:SUBLANES, :]

    h_ref[...] = lax.fori_loop(0, tt // SUBLANES, carry, h_ref[...], unroll=8)
    y_ref[0] = (_gelu_tanh(gate) * b_ref[...]).astype(y_ref.dtype)


def _lru_mixer(x2d, g_in, shape, w_in, conv_w, conv_b, w_gates, b_gates, lam):
    bsz, t, d = shape
    width = lam.shape[0]
    tile = pl.BlockSpec((1, TIME_TILE, d), lambda b, i: (b, i, 0))
    y = pl.pallas_call(
        _lru_kernel,
        grid=(bsz, t // TIME_TILE),
        in_specs=[tile, _resident((1, d)), _resident(w_in.shape), _resident(conv_w.shape), _resident((1, width)),
                  _resident(w_gates.shape), _resident(b_gates.shape), _resident((1, width))],
        out_specs=pl.BlockSpec((1, TIME_TILE, width), lambda b, i: (b, i, 0)),
        out_shape=jax.ShapeDtypeStruct((bsz, t, width), BF16),
        scratch_shapes=[pltpu.VMEM((SUBLANES, width), F32), pltpu.VMEM((1, width), F32),
                        pltpu.VMEM((TIME_TILE, width), F32), pltpu.VMEM((TIME_TILE, width), F32)],
        compiler_params=_params(2, 48),
        name="lru",
    )(x2d.reshape(bsz, t, d), g_in.reshape(1, d), w_in.astype(BF16), conv_w, conv_b.reshape(1, width),
      w_gates.astype(BF16), b_gates, lam.reshape(1, width))
    return y.reshape(bsz * t, width)


def kernel(x, norm_g, ffn_w_in, ffn_w_out, gdn_w_in, gdn_conv_w, gdn_a_log, gdn_dt_bias, gdn_norm_w, gdn_w_out, rwkv_mu, rwkv_w_rkv, rwkv_w0, rwkv_w1, rwkv_w2, rwkv_a0, rwkv_a1, rwkv_a2, rwkv_g1, rwkv_g2, rwkv_k_k, rwkv_k_a, rwkv_r_k, rwkv_ln_w, rwkv_ln_b, rwkv_w_out, ssd_w_in, ssd_conv_w, ssd_conv_b, ssd_dt_bias, ssd_a_log, ssd_d, ssd_norm_w, ssd_w_out, lru_w_in, lru_conv_w, lru_conv_b, lru_w_gates, lru_b_gates, lru_lam, lru_w_out):
    shape = x.shape
    bsz, t, d = shape
    depth = norm_g.shape[0]
    n_mixers = 4
    x2d = x.reshape(bsz * t, d)
    for i in range(depth):
        m, j = i % n_mixers, i // n_mixers
        g = norm_g[i]
        x2d = _ffn(x2d, g[0], g[1], ffn_w_in, ffn_w_out, (i, 0))
        if m == 0:
            y = _gdn_mixer(x2d, g[2], shape, gdn_w_in[j], gdn_conv_w[j], gdn_a_log[j], gdn_dt_bias[j], gdn_norm_w[j])
            w_o = gdn_w_out[j]
        elif m == 1:
            y = _rwkv_mixer(x2d, g[2], shape, rwkv_mu[j], rwkv_w_rkv[j], rwkv_w0[j], rwkv_w1[j], rwkv_w2[j],
                            rwkv_a0[j], rwkv_a1[j], rwkv_a2[j], rwkv_g1[j], rwkv_g2[j], rwkv_k_k[j], rwkv_k_a[j],
                            rwkv_r_k[j], rwkv_ln_w[j], rwkv_ln_b[j])
            w_o = rwkv_w_out[j]
        elif m == 2:
            y = _ssd_mixer(x2d, g[2], shape, ssd_w_in[j], ssd_conv_w[j], ssd_conv_b[j], ssd_dt_bias[j],
                           ssd_a_log[j], ssd_d[j], ssd_norm_w[j])
            w_o = ssd_w_out[j]
        else:
            y = _lru_mixer(x2d, g[2], shape, lru_w_in[j], lru_conv_w[j], lru_conv_b[j], lru_w_gates[j],
                           lru_b_gates[j], lru_lam[j])
            w_o = lru_w_out[j]
        x2d = _mix_out_ffn(y, w_o.astype(BF16), g[3], x2d, g[4], g[5], ffn_w_in, ffn_w_out, (i, 1))
    return x2d.reshape(shape)
```

```python
import functools

import jax
import jax.numpy as jnp
from jax import lax
from jax.experimental import pallas as pl
from jax.experimental.pallas import tpu as pltpu

F32 = jnp.float32
BF16 = jnp.bfloat16

V7X_VMEM_BYTES = 64 * 1024 * 1024
SUBLANES = 8
LANES = 128
MXU_WIDTH = 256

NORM_EPS = 1e-6
L2_EPS = 1e-6
SSD_NORM_EPS = 1e-5
RWKV_GN_EPS = 64e-5
LRU_C = 8.0

ROW_TILE = 512
TIME_TILE = 256
GDN_CHUNK = 64
RWKV_CHUNK = 64
SSD_CHUNK = 128
SSD_TIME_TILE = 512
LRU_TIME_TILE = 512
SSD_GROUPS = 4
CONV_TAPS = 4
WEIGHT_PIECES = 8


def _params(n_axes):
    return pltpu.CompilerParams(
        dimension_semantics=("arbitrary",) * n_axes,
        vmem_limit_bytes=V7X_VMEM_BYTES // 8 * 7,
    )


def _resident(shape):
    zeros = (0,) * len(shape)
    return pl.BlockSpec(shape, lambda *_: zeros, pipeline_mode=pl.Buffered(1))


def _rms(x, g, eps):
    return x * lax.rsqrt(jnp.mean(x * x, axis=-1, keepdims=True) + eps) * g


def _sigmoid(x):
    return 1.0 / (1.0 + jnp.exp(-x))


def _silu(x):
    return x * _sigmoid(x)


def _softplus(x):
    return jnp.maximum(x, 0.0) + jnp.log1p(jnp.exp(-jnp.abs(x)))


def _gelu_tanh(x):
    return 0.5 * x * (1.0 + jnp.tanh(0.7978845608028654 * (x + 0.044715 * (x * x * x))))


def _dot(a, b):
    return jnp.dot(a.astype(BF16), b.astype(BF16), preferred_element_type=F32)


def _dot_nt(a, b):
    return lax.dot_general(a.astype(BF16), b.astype(BF16), (((1,), (1,)), ((), ())),
                           preferred_element_type=F32)


def _dot_tn(a, b):
    return lax.dot_general(a.astype(BF16), b.astype(BF16), (((0,), (0,)), ((), ())),
                           preferred_element_type=F32)


def _seg_cumsum(x, axis, seg):
    pos = lax.broadcasted_iota(jnp.int32, x.shape, axis) & (seg - 1)
    s = 1
    while s < seg:
        x = x + jnp.where(pos >= s, pltpu.roll(x, s, axis), 0.0)
        s *= 2
    return x


def _causal_conv(pre, tail_ref, w):
    tt = pre.shape[0]
    ext = jnp.concatenate([tail_ref[...], pre], axis=0)
    y = pre * w[CONV_TAPS - 1:CONV_TAPS, :]
    for k in range(1, CONV_TAPS):
        y = y + pltpu.roll(ext, k, 0)[SUBLANES:, :] * w[CONV_TAPS - 1 - k:CONV_TAPS - k, :]
    tail_ref[...] = pre[tt - SUBLANES:, :]
    return y


def _tri_inv(lows, block):
    n = lows[0].shape[0]
    n_sys = n // block
    row = lax.broadcasted_iota(jnp.int32, (n, n), 0)
    col = lax.broadcasted_iota(jnp.int32, (n, n), 1)
    on_diag = (row >> (block.bit_length() - 1)) == (col >> (block.bit_length() - 1))

    def to_diag(m_l):
        return jnp.where(on_diag, jnp.concatenate([m_l] * n_sys, axis=0), 0.0)

    row_l = lax.broadcasted_iota(jnp.int32, (block, n), 0)
    col_l = lax.broadcasted_iota(jnp.int32, (block, n), 1) & (block - 1)
    eye_l = jnp.where(row_l == col_l, 1.0, 0.0)
    pair_l = (row_l >> 1) == (col_l >> 1)
    t_ls = []
    for low in lows:
        low_l = low[:block]
        for s in range(1, n_sys):
            low_l = low_l + low[s * block:(s + 1) * block]
        t_ls.append(eye_l - jnp.where(pair_l, low_l, 0.0))
    b, sh = 2, 1
    while b < block:
        sub = ((row >> (sh + 1)) == (col >> (sh + 1))) & (((row >> sh) & 1) == 1) & (((col >> sh) & 1) == 0)
        xs = [_dot(t_l, jnp.where(sub, low, 0.0)) for t_l, low in zip(t_ls, lows)]
        t_ls = [t_l - _dot(x, to_diag(t_l)) for t_l, x in zip(t_ls, xs)]
        b, sh = 2 * b, sh + 1
    return [to_diag(t_l) for t_l in t_ls]


def _ffn_half_step(x, g0, g1, win_ref, wout_ref):
    d_ff = wout_ref.shape[0]
    mid = (d_ff // MXU_WIDTH // 2) * MXU_WIDTH
    xn = _rms(x, g0, NORM_EPS).astype(BF16)
    acc = None
    for lo, hi in ((0, mid), (mid, d_ff)) if 0 < mid < d_ff else ((0, d_ff),):
        gate = jnp.dot(xn, win_ref[:, lo:hi], preferred_element_type=F32)
        up = jnp.dot(xn, win_ref[:, d_ff + lo:d_ff + hi], preferred_element_type=F32)
        act = (_silu(gate) * up).astype(BF16)
        part = jnp.dot(act, wout_ref[lo:hi, :], preferred_element_type=F32)
        acc = part if acc is None else acc + part
    return x + _rms(acc, 0.5 * g1, NORM_EPS)


def _stage_weight(src_hbm, idx, dst_ref, stage_ref, sem_ref):
    rows = stage_ref.shape[1]

    def copy(c):
        piece = src_hbm.at[idx[0], idx[1], pl.ds(c * rows, rows)]
        return pltpu.make_async_copy(piece, stage_ref.at[c % 2], sem_ref.at[c % 2])

    n_pieces = dst_ref.shape[0] // rows
    copy(0).start()
    for c in range(n_pieces):
        if c + 1 < n_pieces:
            copy(c + 1).start()
        copy(c).wait()
        dst_ref[c * rows:(c + 1) * rows, :] = stage_ref[c % 2].astype(BF16)


def _ffn_weight_scratch(w_in, w_out):
    (d, f2), (f, _) = w_in.shape[2:], w_out.shape[2:]
    return [pltpu.VMEM((d, f2), BF16), pltpu.VMEM((f, d), BF16),
            pltpu.VMEM((2, d // WEIGHT_PIECES, f2), F32), pltpu.VMEM((2, f // WEIGHT_PIECES, d), F32),
            pltpu.SemaphoreType.DMA((2,)), pltpu.SemaphoreType.DMA((2,))]


def _load_ffn_weights(win_hbm, wout_hbm, idx, win_ref, wout_ref, sin_ref, sout_ref, semi_ref, semo_ref):
    @pl.when(pl.program_id(0) == 0)
    def _load():
        _stage_weight(win_hbm, idx, win_ref, sin_ref, semi_ref)
        _stage_weight(wout_hbm, idx, wout_ref, sout_ref, semo_ref)


def _ffn_kernel(x_ref, g0_ref, g1_ref, win_hbm, wout_hbm, o_ref, win_ref, wout_ref, *stage, idx):
    _load_ffn_weights(win_hbm, wout_hbm, idx, win_ref, wout_ref, *stage)
    o_ref[...] = _ffn_half_step(x_ref[...], g0_ref[...], g1_ref[...], win_ref, wout_ref)


def _ffn(x2d, g0, g1, w_in, w_out, idx):
    n, d = x2d.shape
    row = pl.BlockSpec((ROW_TILE, d), lambda i: (i, 0))
    hbm = pl.BlockSpec(memory_space=pl.ANY)
    return pl.pallas_call(
        functools.partial(_ffn_kernel, idx=idx),
        grid=(n // ROW_TILE,),
        in_specs=[row, _resident((1, d)), _resident((1, d)), hbm, hbm],
        out_specs=row,
        out_shape=jax.ShapeDtypeStruct((n, d), F32),
        scratch_shapes=_ffn_weight_scratch(w_in, w_out),
        compiler_params=_params(1),
        name="ffn",
    )(x2d, g0.reshape(1, d), g1.reshape(1, d), w_in, w_out)


def _mix_out_ffn_kernel(y_ref, wo_ref, go_ref, x_ref, g0_ref, g1_ref, win_hbm, wout_hbm, o_ref, win_ref, wout_ref,
                        *stage, idx):
    _load_ffn_weights(win_hbm, wout_hbm, idx, win_ref, wout_ref, *stage)
    h = jnp.dot(y_ref[...], wo_ref[...], preferred_element_type=F32)
    x = x_ref[...] + _rms(h, go_ref[...], NORM_EPS)
    o_ref[...] = _ffn_half_step(x, g0_ref[...], g1_ref[...], win_ref, wout_ref)


def _mix_out_ffn(y2d, w_o, g_o, x2d, g0, g1, w_in, w_out, idx):
    n, d = x2d.shape
    k = y2d.shape[1]
    row = pl.BlockSpec((ROW_TILE, d), lambda i: (i, 0))
    vec = _resident((1, d))
    hbm = pl.BlockSpec(memory_space=pl.ANY)
    return pl.pallas_call(
        functools.partial(_mix_out_ffn_kernel, idx=idx),
        grid=(n // ROW_TILE,),
        in_specs=[pl.BlockSpec((ROW_TILE, k), lambda i: (i, 0)), _resident(w_o.shape), vec, row, vec, vec, hbm, hbm],
        out_specs=row,
        out_shape=jax.ShapeDtypeStruct((n, d), F32),
        scratch_shapes=_ffn_weight_scratch(w_in, w_out),
        compiler_params=_params(1),
        name="mix_out_ffn",
    )(y2d, w_o, g_o.reshape(1, d), x2d, g0.reshape(1, d), g1.reshape(1, d), w_in, w_out)


def _gdn_chunk_local(q, k, v, beta, gc_col, gc_row, causal, strict):
    decay = jnp.exp(jnp.where(causal, gc_col - gc_row, -jnp.inf))
    lower = jnp.where(strict, _dot_nt(k, k) * decay * beta, 0.0)
    eg = jnp.exp(gc_col)
    rhs = jnp.concatenate([v * beta, k * (beta * eg)], axis=1)
    return lower, rhs, _dot_nt(q, k) * decay, q * eg


def _gdn_kernel(x_ref, gn_ref, w_ref, wbat_ref, cw_ref, alc_ref, dtc_ref, alr_ref, dtr_ref, nw_ref, y_ref,
                tail_ref, s_ref, *, heads, dk, dv, chunk):
    @pl.when(pl.program_id(1) == 0)
    def _init():
        tail_ref[...] = jnp.zeros_like(tail_ref)
        s_ref[...] = jnp.zeros_like(s_ref)

    n_qkv = cw_ref.shape[1]
    o_k, o_v, o_ba = heads * dk, 2 * heads * dk, n_qkv + heads * dv
    xn = _rms(x_ref[0], gn_ref[...], NORM_EPS).astype(BF16)
    qkv = _silu(_causal_conv(jnp.dot(xn, w_ref[:, :n_qkv], preferred_element_type=F32), tail_ref, cw_ref[...]))
    z = jnp.dot(xn, w_ref[:, n_qkv:o_ba], preferred_element_type=F32)
    ba_c = jnp.dot(xn, w_ref[:, o_ba:], preferred_element_type=F32)
    ba_r = _dot_nt(wbat_ref[...], xn)
    beta_all = _sigmoid(ba_c)
    gc_cols = _seg_cumsum(-jnp.exp(alc_ref[...]) * _softplus(ba_c + dtc_ref[...]), 0, chunk)
    gc_rows = _seg_cumsum(-jnp.exp(alr_ref[...]) * _softplus(ba_r[heads:, :] + dtr_ref[...]), 1, chunk)
    tt = qkv.shape[0]
    shift = chunk.bit_length() - 1
    row = lax.broadcasted_iota(jnp.int32, (tt, tt), 0)
    col = lax.broadcasted_iota(jnp.int32, (tt, tt), 1)
    same_chunk = (row >> shift) == (col >> shift)
    causal = same_chunk & (row >= col)
    strict = same_chunk & (row > col)
    hds = []
    for h in range(heads):
        qh = qkv[:, h * dk:(h + 1) * dk]
        kh = qkv[:, o_k + h * dk:o_k + (h + 1) * dk]
        vh = qkv[:, o_v + h * dv:o_v + (h + 1) * dv]
        qh = qh * (lax.rsqrt(jnp.sum(qh * qh, axis=-1, keepdims=True) + L2_EPS) * dk ** -0.5)
        kh = kh * lax.rsqrt(jnp.sum(kh * kh, axis=-1, keepdims=True) + L2_EPS)
        beta = beta_all[:, h:h + 1]
        gc_col = gc_cols[:, heads + h:heads + h + 1]
        gc_row = gc_rows[h:h + 1, :]
        lower, rhs, qk, q_dec = _gdn_chunk_local(qh, kh, vh, beta, gc_col, gc_row, causal, strict)
        hds.append(dict(lower=lower, rhs=rhs, qk=qk, q_dec=q_dec, k=kh, gc=gc_col, state=s_ref[h],
                        v_new=[], o_state=[]))
    for hd, t_inv in zip(hds, _tri_inv([hd["lower"] for hd in hds], chunk)):
        sol = _dot(t_inv, hd["rhs"])
        hd.update(u=sol[:, :dv], w=sol[:, dv:])

    for c in range(tt // chunk):
        sl = slice(c * chunk, (c + 1) * chunk)
        for hd in hds:
            g_last = hd["gc"][(c + 1) * chunk - 1:(c + 1) * chunk, :]
            vn = hd["u"][sl] - _dot(hd["w"][sl], hd["state"])
            hd["o_state"].append(_dot(hd["q_dec"][sl], hd["state"]))
            hd["state"] = hd["state"] * jnp.exp(g_last) + _dot_tn(hd["k"][sl] * jnp.exp(g_last - hd["gc"][sl]), vn)
            hd["v_new"].append(vn)

    for h, hd in enumerate(hds):
        s_ref[h] = hd["state"]
        o = jnp.concatenate(hd["o_state"], axis=0) + _dot(hd["qk"], jnp.concatenate(hd["v_new"], axis=0))
        o = _rms(o, nw_ref[...], NORM_EPS) * _silu(z[:, h * dv:(h + 1) * dv])
        y_ref[0, :, h * dv:(h + 1) * dv] = o.astype(y_ref.dtype)


def _gdn_mixer(x2d, g_in, shape, w_in, conv_w, a_log, dt_bias, norm_w):
    bsz, t, d = shape
    heads = a_log.shape[0]
    dv = norm_w.shape[0]
    n_qkv = conv_w.shape[1]
    dk = (n_qkv - heads * dv) // (2 * heads)
    w = w_in.astype(BF16)
    w_ba_t = w[:, n_qkv + heads * dv:].T
    unused = jnp.zeros_like(a_log)
    tile = pl.BlockSpec((1, TIME_TILE, d), lambda b, i: (b, i, 0))
    y = pl.pallas_call(
        functools.partial(_gdn_kernel, heads=heads, dk=dk, dv=dv, chunk=GDN_CHUNK),
        grid=(bsz, t // TIME_TILE),
        in_specs=[tile, _resident((1, d)), _resident(w.shape), _resident(w_ba_t.shape), _resident(conv_w.shape),
                  _resident((1, 2 * heads)), _resident((1, 2 * heads)), _resident((heads, 1)), _resident((heads, 1)),
                  _resident((1, dv))],
        out_specs=pl.BlockSpec((1, TIME_TILE, heads * dv), lambda b, i: (b, i, 0)),
        out_shape=jax.ShapeDtypeStruct((bsz, t, heads * dv), BF16),
        scratch_shapes=[pltpu.VMEM((SUBLANES, n_qkv), F32), pltpu.VMEM((heads, dk, dv), F32)],
        compiler_params=_params(2),
        name="gdn",
    )(x2d.reshape(bsz, t, d), g_in.reshape(1, d), w, w_ba_t, conv_w,
      jnp.concatenate([unused, a_log]).reshape(1, 2 * heads), jnp.concatenate([unused, dt_bias]).reshape(1, 2 * heads),
      a_log.reshape(heads, 1), dt_bias.reshape(heads, 1), norm_w.reshape(1, dv))
    return y.reshape(bsz * t, heads * dv)


def _rwkv_project(x_ref, g_ref, mu_ref, wrkv_ref, w0_ref, w1_ref, w2_ref, a0_ref, a1_ref, a2_ref,
                  g1_ref, g2_ref, r_ref, k_ref, v_ref, w_ref, a_ref, go_ref, prev_ref):
    hn = _rms(x_ref[0], g_ref[...], NORM_EPS)
    tm = hn.shape[0]
    ext = jnp.concatenate([prev_ref[...], hn], axis=0)
    xx = pltpu.roll(ext, 1, 0)[SUBLANES:, :] - hn
    prev_ref[...] = hn[tm - SUBLANES:, :]
    mu = mu_ref[...]

    def mix(i):
        return (hn + xx * mu[i:i + 1, :]).astype(BF16)

    r_ref[0] = jnp.dot(mix(0), wrkv_ref[0], preferred_element_type=F32)
    k_ref[0] = jnp.dot(mix(1), wrkv_ref[1], preferred_element_type=F32)
    v_ref[0] = jnp.dot(mix(2), wrkv_ref[2], preferred_element_type=F32)
    w_ref[0] = w0_ref[...] + _dot(jnp.tanh(jnp.dot(mix(3), w1_ref[...], preferred_element_type=F32)), w2_ref[...])
    a_ref[0] = a0_ref[...] + _dot(jnp.dot(mix(4), a1_ref[...], preferred_element_type=F32), a2_ref[...])
    go_ref[0] = _dot(_sigmoid(jnp.dot(mix(5), g1_ref[...], preferred_element_type=F32)), g2_ref[...])


def _rwkv_chunk_local(r, k, v, w_pre, a_pre, kk_w, ka_w, *, hd, chunk):
    tt, width = r.shape
    n_heads = width // hd
    lane = lax.broadcasted_iota(jnp.int32, (1, width), 1)
    head_of_lane = [(lane >= j * hd) & (lane < (j + 1) * hd) for j in range(n_heads)]

    def by_head(parts):
        out = parts[-1]
        for m, p in zip(head_of_lane[:-1], parts[:-1]):
            out = jnp.where(m, p, out)
        return out

    log_w = -jnp.exp(-_softplus(-w_pre) - 0.5)
    a = _sigmoid(a_pre)
    kk = k * kk_w
    kk = kk * lax.rsqrt(_head_sum(kk * kk, hd) + L2_EPS)
    k2 = k * (1.0 + (a - 1.0) * ka_w)
    cum = _seg_cumsum(log_w, 0, chunk)
    inv = jnp.exp(-cum)
    a_t = -kk * jnp.exp(cum - log_w)
    b_t = kk * a * inv
    k_t = k2 * inv
    r_t = r * jnp.exp(cum)

    n_chunks = tt // chunk
    lane_l = lax.broadcasted_iota(jnp.int32, (chunk, LANES), 1)
    row_l = lax.broadcasted_iota(jnp.int32, (chunk, LANES), 0)
    col_l = lane_l & (chunk - 1)
    upper_half = lane_l >= chunk
    zeros = jnp.zeros((chunk, LANES), F32)
    v_swapped = jnp.concatenate([v[(c ^ 1) * chunk:((c ^ 1) + 1) * chunk] for c in range(n_chunks)], axis=0)

    def place(block, c):
        return jnp.concatenate([block if col == c // 2 else zeros for col in range(n_chunks // 2)], axis=1)

    lows, rhs, y0_parts, p_rb = [], [], [], []
    for m in head_of_lane:
        a_m, r_m = jnp.where(m, a_t, 0.0), jnp.where(m, r_t, 0.0)
        ab, ak, rb, rk = [], [], [], []
        for c in range(n_chunks):
            sl = slice(c * chunk, (c + 1) * chunk)
            right = [b_t[sl], k_t[sl]] if c % 2 == 0 else [k_t[sl], b_t[sl]]
            g = _dot_nt(jnp.concatenate([a_m[sl], r_m[sl]], axis=0), jnp.concatenate(right, axis=0))
            b_half = upper_half if c % 2 else ~upper_half
            top, bot = g[:chunk], g[chunk:]
            ab.append(place(jnp.where(b_half & (row_l > col_l), -top, 0.0), c))
            ak.append(place(jnp.where(~b_half & (row_l > col_l), top, 0.0), c))
            rb.append(place(jnp.where(b_half & (row_l >= col_l), bot, 0.0), c))
            rk.append(place(jnp.where(~b_half & (row_l >= col_l), bot, 0.0), c))
        lows.append(jnp.concatenate(ab, axis=0))
        p_rb.append(jnp.concatenate(rb, axis=0))
        rhs.append(jnp.concatenate([_dot(jnp.concatenate(ak, axis=0), v_swapped), a_t], axis=1))
        y0_parts.append(_dot(jnp.concatenate(rk, axis=0), v_swapped))
    return dict(lows=lows, rhs=rhs, y0=by_head(y0_parts), p_rb=p_rb, by_head=by_head,
                b_t=b_t, k_t=k_t, r_t=r_t, cum=cum, k2=k2, r=r, v=v)


def _head_sum(x, hd):
    lane = lax.broadcasted_iota(jnp.int32, (1, x.shape[1]), 1)
    tot = jnp.zeros_like(x)
    for j in range(x.shape[1] // hd):
        m = (lane >= j * hd) & (lane < (j + 1) * hd)
        tot = tot + jnp.where(m, jnp.sum(jnp.where(m, x, 0.0), axis=-1, keepdims=True), 0.0)
    return tot


def _rwkv_kernel(x_ref, gn_ref, mu_ref, wrkv_ref, w0_ref, w1_ref, w2_ref, a0_ref, a1_ref, a2_ref, g1_ref, g2_ref,
                 kk_ref, ka_ref, rk_ref, lnw_ref, lnb_ref, y_ref,
                 prev_ref, s_ref, r_ref, k_ref, v_ref, w_ref, a_ref, g_ref, *, hd, chunk):
    @pl.when(pl.program_id(1) == 0)
    def _init():
        prev_ref[...] = jnp.zeros_like(prev_ref)
        s_ref[...] = jnp.zeros_like(s_ref)

    _rwkv_project(x_ref, gn_ref, mu_ref, wrkv_ref, w0_ref, w1_ref, w2_ref, a0_ref, a1_ref, a2_ref, g1_ref, g2_ref,
                  r_ref, k_ref, v_ref, w_ref, a_ref, g_ref, prev_ref)
    _rwkv_recurrence(r_ref, k_ref, v_ref, w_ref, a_ref, g_ref, kk_ref, ka_ref, rk_ref, lnw_ref, lnb_ref,
                     y_ref, s_ref, hd=hd, chunk=chunk)


def _rwkv_recurrence(r_ref, k_ref, v_ref, w_ref, a_ref, g_ref, kk_ref, ka_ref, rk_ref, lnw_ref, lnb_ref,
                     y_ref, s_ref, *, hd, chunk):
    tt, width = r_ref.shape[1:]
    srow = lax.broadcasted_iota(jnp.int32, (LANES, LANES), 0)
    scol = lax.broadcasted_iota(jnp.int32, (LANES, LANES), 1)
    same_head = None
    for j in range(LANES // hd):
        blk = (srow >= j * hd) & (srow < (j + 1) * hd) & (scol >= j * hd) & (scol < (j + 1) * hd)
        same_head = blk if same_head is None else (same_head | blk)

    groups = []
    for grp in range(width // LANES):
        ls = slice(grp * LANES, (grp + 1) * LANES)
        loc = _rwkv_chunk_local(r_ref[0, :, ls], k_ref[0, :, ls], v_ref[0, :, ls], w_ref[0, :, ls], a_ref[0, :, ls],
                                kk_ref[:, ls], ka_ref[:, ls], hd=hd, chunk=chunk)
        loc.update(state=s_ref[grp], us=[], rs=[])
        groups.append(loc)
    t_invs = _tri_inv([low for gp in groups for low in gp["lows"]], chunk)
    for gi, gp in enumerate(groups):
        n_h = len(gp["lows"])
        sols = [_dot(t_inv, rhs) for t_inv, rhs in zip(t_invs[gi * n_h:(gi + 1) * n_h], gp["rhs"])]
        gp.update(u0=gp["by_head"]([s[:, :LANES] for s in sols]), w_t=gp["by_head"]([s[:, LANES:] for s in sols]))

    for c in range(tt // chunk):
        sl = slice(c * chunk, (c + 1) * chunk)
        for gp in groups:
            u_c = gp["u0"][sl] + _dot_nt(gp["w_t"][sl], gp["state"])
            gp["rs"].append(_dot_nt(gp["r_t"][sl], gp["state"]))
            decay_end = jnp.exp(gp["cum"][(c + 1) * chunk - 1:(c + 1) * chunk, :])
            upd = gp["state"] + _dot_tn(u_c, gp["b_t"][sl]) + _dot_tn(gp["v"][sl], gp["k_t"][sl])
            gp["state"] = jnp.where(same_head, upd, 0.0) * decay_end
            gp["us"].append(u_c)

    for grp, gp in enumerate(groups):
        ls = slice(grp * LANES, (grp + 1) * LANES)
        s_ref[grp] = gp["state"]
        u = jnp.concatenate(gp["us"], axis=0)
        y = jnp.concatenate(gp["rs"], axis=0) + gp["y0"] + gp["by_head"]([_dot(p, u) for p in gp["p_rb"]])
        mean = _head_sum(y, hd) * (1.0 / hd)
        yc = y - mean
        var = _head_sum(yc * yc, hd) * (1.0 / hd)
        y = yc * lax.rsqrt(var + RWKV_GN_EPS) * lnw_ref[:, ls] + lnb_ref[:, ls]
        bonus = _head_sum(gp["r"] * gp["k2"] * rk_ref[:, ls], hd) * gp["v"]
        y_ref[0, :, ls] = ((y + bonus) * g_ref[0, :, ls]).astype(y_ref.dtype)


def _rwkv_mixer(x2d, g_in, shape, mu, w_rkv, w0, w1, w2, a0, a1, a2, g1, g2, k_k, k_a, r_k, ln_w, ln_b):
    bsz, t, d = shape
    hd = r_k.shape[1]
    tile = pl.BlockSpec((1, TIME_TILE, d), lambda b, i: (b, i, 0))
    vec = _resident((1, d))
    proj = pltpu.VMEM((1, TIME_TILE, d), F32)
    y = pl.pallas_call(
        functools.partial(_rwkv_kernel, hd=hd, chunk=RWKV_CHUNK),
        grid=(bsz, t // TIME_TILE),
        in_specs=[tile, vec, _resident(mu.shape), _resident(w_rkv.shape), vec, _resident(w1.shape),
                  _resident(w2.shape), vec, _resident(a1.shape), _resident(a2.shape), _resident(g1.shape),
                  _resident(g2.shape)] + [vec] * 5,
        out_specs=tile,
        out_shape=jax.ShapeDtypeStruct((bsz, t, d), BF16),
        scratch_shapes=[pltpu.VMEM((SUBLANES, d), F32), pltpu.VMEM((d // LANES, LANES, LANES), F32)] + [proj] * 6,
        compiler_params=_params(2),
        name="rwkv",
    )(x2d.reshape(bsz, t, d), g_in.reshape(1, d), mu, w_rkv.astype(BF16), w0.reshape(1, d), w1.astype(BF16),
      w2.astype(BF16), a0.reshape(1, d), a1.astype(BF16), a2.astype(BF16), g1.astype(BF16), g2.astype(BF16),
      k_k.reshape(1, d), k_a.reshape(1, d), r_k.reshape(1, d), ln_w.reshape(1, d), ln_b.reshape(1, d))
    return y.reshape(bsz * t, d)


def _ssd_kernel(x_ref, gn_ref, w_ref, wdtt_ref, cw_ref, cb_ref, dtbc_ref, dtbr_ref, alc_ref, alr_ref, dskip_ref, nw_ref,
                y_ref, tail_ref, h_ref, *, groups, gh, hp, ns, chunk):
    @pl.when(pl.program_id(1) == 0)
    def _init():
        tail_ref[...] = jnp.zeros_like(tail_ref)
        h_ref[...] = jnp.zeros_like(h_ref)

    gw = gh * hp
    inner = groups * gw
    n_xbc = cw_ref.shape[1]
    xn = _rms(x_ref[0], gn_ref[...], NORM_EPS).astype(BF16)
    z = jnp.dot(xn, w_ref[:, :inner], preferred_element_type=F32)
    xbc = jnp.dot(xn, w_ref[:, inner:inner + n_xbc], preferred_element_type=F32)
    xbc = _silu(_causal_conv(xbc, tail_ref, cw_ref[...]) + cb_ref[...])
    dt_c = _softplus(jnp.dot(xn, w_ref[:, inner + n_xbc:], preferred_element_type=F32) + dtbc_ref[...])
    dt_r = _softplus(_dot_nt(wdtt_ref[...], xn) + dtbr_ref[...])
    acs_c = _seg_cumsum(dt_c * -jnp.exp(alc_ref[...]), 0, chunk)
    acs_r = _seg_cumsum(dt_r * -jnp.exp(alr_ref[...]), 1, chunk)
    tt = xbc.shape[0]

    n_heads = groups * gh
    head_row = lax.broadcasted_iota(jnp.int32, (n_heads, inner), 0)
    head_of_lane = lax.broadcasted_iota(jnp.int32, (n_heads, inner), 1) >> (hp.bit_length() - 1)
    spread = jnp.where(head_row == head_of_lane, 1.0, 0.0).astype(BF16)

    def expand(cols):
        rows = cols.shape[0]
        hi = cols.astype(BF16)
        rest = cols - hi.astype(F32)
        mid = rest.astype(BF16)
        lo = (rest - mid.astype(F32)).astype(BF16)
        parts = jnp.dot(jnp.concatenate([hi, mid, lo], axis=0), spread, preferred_element_type=F32)
        return parts[:rows] + parts[rows:2 * rows] + parts[2 * rows:]

    rowi = lax.broadcasted_iota(jnp.int32, (chunk, chunk), 0)
    coli = lax.broadcasted_iota(jnp.int32, (chunk, chunk), 1)
    causal = rowi >= coli
    pair_lo = lax.broadcasted_iota(jnp.int32, (1, 2 * hp), 1) < hp

    states = [h_ref[g] for g in range(groups)]
    ys = [[] for _ in range(groups)]
    for c in range(tt // chunk):
        sl = slice(c * chunk, (c + 1) * chunk)
        ac_c = acs_c[sl]
        ac_r = acs_r[:, sl]
        tot = ac_c[chunk - 1:chunk, :]
        dt_x = expand(dt_c[sl])
        in_decay_x = expand(jnp.exp(tot - ac_c))
        out_decay_x = expand(jnp.exp(ac_c))
        for g in range(groups):
            first = g * gh
            gl = slice(g * gw, (g + 1) * gw)
            xc = xbc[sl, gl]
            bc = xbc[sl, inner + g * ns:inner + (g + 1) * ns]
            cc = xbc[sl, inner + groups * ns + g * ns:inner + groups * ns + (g + 1) * ns]
            xdt = xc * dt_x[:, gl]
            cb = _dot_nt(cc, bc)
            y_pairs = []
            for p in range(gh // 2):
                xp = xdt[:, 2 * p * hp:2 * (p + 1) * hp]
                halves = []
                for e in (first + 2 * p, first + 2 * p + 1):
                    seg = jnp.where(causal, ac_c[:, e:e + 1] - ac_r[e:e + 1, :], -jnp.inf)
                    halves.append(_dot(cb * jnp.exp(seg), xp))
                y_pairs.append(jnp.where(pair_lo, halves[0], halves[1]))
            y_diag = jnp.concatenate(y_pairs, axis=1)
            y_off = _dot(cc, states[g]) * out_decay_x[:, gl]
            new = _dot_tn(bc, xdt * in_decay_x[:, gl])
            states[g] = states[g] * out_decay_x[chunk - 1:chunk, gl] + new
            ys[g].append(y_diag + y_off + xc * dskip_ref[:, gl])
    for g in range(groups):
        h_ref[g] = states[g]
        ls = slice(g * gw, (g + 1) * gw)
        y = jnp.concatenate(ys[g], axis=0) * _silu(z[:, ls])
        y_ref[0, :, ls] = _rms(y, nw_ref[:, ls], SSD_NORM_EPS).astype(y_ref.dtype)


def _ssd_mixer(x2d, g_in, shape, w_in, conv_w, conv_b, dt_bias, a_log, d_skip, norm_w):
    bsz, t, d = shape
    n_heads = a_log.shape[0]
    inner = norm_w.shape[0]
    hp = inner // n_heads
    n_xbc = conv_w.shape[1]
    groups = SSD_GROUPS
    ns = (n_xbc - inner) // (2 * groups)
    gh = n_heads // groups
    w = w_in.astype(BF16)
    w_dt_t = w[:, inner + n_xbc:].T
    tile = SSD_TIME_TILE
    y = pl.pallas_call(
        functools.partial(_ssd_kernel, groups=groups, gh=gh, hp=hp, ns=ns, chunk=SSD_CHUNK),
        grid=(bsz, t // tile),
        in_specs=[pl.BlockSpec((1, tile, d), lambda b, i: (b, i, 0)), _resident((1, d)), _resident(w.shape),
                  _resident(w_dt_t.shape), _resident(conv_w.shape), _resident((1, n_xbc)),
                  _resident((1, n_heads)), _resident((n_heads, 1)), _resident((1, n_heads)), _resident((n_heads, 1)),
                  _resident((1, inner)), _resident((1, inner))],
        out_specs=pl.BlockSpec((1, tile, inner), lambda b, i: (b, i, 0)),
        out_shape=jax.ShapeDtypeStruct((bsz, t, inner), BF16),
        scratch_shapes=[pltpu.VMEM((SUBLANES, n_xbc), F32), pltpu.VMEM((groups, ns, inner // groups), F32)],
        compiler_params=_params(2),
        name="ssd",
    )(x2d.reshape(bsz, t, d), g_in.reshape(1, d), w, w_dt_t, conv_w, conv_b.reshape(1, n_xbc),
      dt_bias.reshape(1, n_heads), dt_bias.reshape(n_heads, 1), a_log.reshape(1, n_heads), a_log.reshape(n_heads, 1),
      jnp.repeat(d_skip, hp).reshape(1, inner), norm_w.reshape(1, inner))
    return y.reshape(bsz * t, inner)


def _lru_kernel(x_ref, gn_ref, w_ref, cw_ref, cb_ref, wg_ref, bg_ref, lam_ref, y_ref, tail_ref, h_ref, a_ref, b_ref):
    @pl.when(pl.program_id(1) == 0)
    def _init():
        tail_ref[...] = jnp.zeros_like(tail_ref)
        h_ref[...] = jnp.zeros_like(h_ref)

    width = lam_ref.shape[1]
    xn = _rms(x_ref[0], gn_ref[...], NORM_EPS).astype(BF16)
    gate = jnp.dot(xn, w_ref[:, :width], preferred_element_type=F32)
    u = _causal_conv(jnp.dot(xn, w_ref[:, width:], preferred_element_type=F32), tail_ref, cw_ref[...]) + cb_ref[...]
    tt = u.shape[0]
    blocks, bw, _ = wg_ref.shape
    groups = (tt // SUBLANES, SUBLANES, bw)
    pos = lax.broadcasted_iota(jnp.int32, groups, 1)
    for n in range(blocks):
        ls = slice(n * bw, (n + 1) * bw)
        un = u[:, ls]
        gates = _dot(un, wg_ref[n])
        r = _sigmoid(gates[:, :bw] + bg_ref[0:1, ls])
        i = _sigmoid(gates[:, bw:] + bg_ref[1:2, ls])
        log_a = -LRU_C * r * _softplus(-lam_ref[:, ls])
        a = jnp.exp(log_a)
        b = jnp.sqrt(1.0 - a * a) * (i * un)
        a, b = a.reshape(groups), b.reshape(groups)
        s = 1
        while s < SUBLANES:
            keep = pos >= s
            b = jnp.where(keep, a * pltpu.roll(b, s, 1) + b, b)
            a = jnp.where(keep, a * pltpu.roll(a, s, 1), a)
            s *= 2
        a_ref[:, ls] = a.reshape(tt, bw)
        b_ref[:, ls] = b.reshape(tt, bw)

    def carry(g, h):
        rows = pl.ds(pl.multiple_of(g * SUBLANES, SUBLANES), SUBLANES)
        hg = b_ref[rows, :] + a_ref[rows, :] * h
        b_ref[rows, :] = hg
        return hg[SUBLANES - 1:SUBLANES, :]

    h_ref[...] = lax.fori_loop(0, tt // SUBLANES, carry, h_ref[...], unroll=8)
    y_ref[0] = (_gelu_tanh(gate) * b_ref[...]).astype(y_ref.dtype)


def _lru_mixer(x2d, g_in, shape, w_in, conv_w, conv_b, w_gates, b_gates, lam):
    bsz, t, d = shape
    width = lam.shape[0]
    tt = LRU_TIME_TILE
    y = pl.pallas_call(
        _lru_kernel,
        grid=(bsz, t // tt),
        in_specs=[pl.BlockSpec((1, tt, d), lambda b, i: (b, i, 0)), _resident((1, d)), _resident(w_in.shape),
                  _resident(conv_w.shape), _resident((1, width)), _resident(w_gates.shape), _resident(b_gates.shape),
                  _resident((1, width))],
        out_specs=pl.BlockSpec((1, tt, width), lambda b, i: (b, i, 0)),
        out_shape=jax.ShapeDtypeStruct((bsz, t, width), BF16),
        scratch_shapes=[pltpu.VMEM((SUBLANES, width), F32), pltpu.VMEM((1, width), F32),
                        pltpu.VMEM((tt, width), F32), pltpu.VMEM((tt, width), F32)],
        compiler_params=_params(2),
        name="lru",
    )(x2d.reshape(bsz, t, d), g_in.reshape(1, d), w_in.astype(BF16), conv_w, conv_b.reshape(1, width),
      w_gates.astype(BF16), b_gates, lam.reshape(1, width))
    return y.reshape(bsz * t, width)


def kernel(x, norm_g, ffn_w_in, ffn_w_out, gdn_w_in, gdn_conv_w, gdn_a_log, gdn_dt_bias, gdn_norm_w, gdn_w_out, rwkv_mu, rwkv_w_rkv, rwkv_w0, rwkv_w1, rwkv_w2, rwkv_a0, rwkv_a1, rwkv_a2, rwkv_g1, rwkv_g2, rwkv_k_k, rwkv_k_a, rwkv_r_k, rwkv_ln_w, rwkv_ln_b, rwkv_w_out, ssd_w_in, ssd_conv_w, ssd_conv_b, ssd_dt_bias, ssd_a_log, ssd_d, ssd_norm_w, ssd_w_out, lru_w_in, lru_conv_w, lru_conv_b, lru_w_gates, lru_b_gates, lru_lam, lru_w_out):
    shape = x.shape
    bsz, t, d = shape
    depth = norm_g.shape[0]
    n_mixers = 4
    x2d = x.reshape(bsz * t, d)
    for i in range(depth):
        m, j = i % n_mixers, i // n_mixers
        g = norm_g[i]
        x2d = _ffn(x2d, g[0], g[1], ffn_w_in, ffn_w_out, (i, 0))
        if m == 0:
            y = _gdn_mixer(x2d, g[2], shape, gdn_w_in[j], gdn_conv_w[j], gdn_a_log[j], gdn_dt_bias[j], gdn_norm_w[j])
            w_o = gdn_w_out[j]
        elif m == 1:
            y = _rwkv_mixer(x2d, g[2], shape, rwkv_mu[j], rwkv_w_rkv[j], rwkv_w0[j], rwkv_w1[j], rwkv_w2[j],
                            rwkv_a0[j], rwkv_a1[j], rwkv_a2[j], rwkv_g1[j], rwkv_g2[j], rwkv_k_k[j], rwkv_k_a[j],
                            rwkv_r_k[j], rwkv_ln_w[j], rwkv_ln_b[j])
            w_o = rwkv_w_out[j]
        elif m == 2:
            y = _ssd_mixer(x2d, g[2], shape, ssd_w_in[j], ssd_conv_w[j], ssd_conv_b[j], ssd_dt_bias[j],
                           ssd_a_log[j], ssd_d[j], ssd_norm_w[j])
            w_o = ssd_w_out[j]
        else:
            y = _lru_mixer(x2d, g[2], shape, lru_w_in[j], lru_conv_w[j], lru_conv_b[j], lru_w_gates[j],
                           lru_b_gates[j], lru_lam[j])
            w_o = lru_w_out[j]
        x2d = _mix_out_ffn(y, w_o.astype(BF16), g[3], x2d, g[4], g[5], ffn_w_in, ffn_w_out, (i, 1))
    return x2d.reshape(shape)
```

```python
import functools

import jax
import jax.numpy as jnp
from jax import lax
from jax.experimental import pallas as pl
from jax.experimental.pallas import tpu as pltpu

F32 = jnp.float32
BF16 = jnp.bfloat16

V7X_VMEM_BYTES = 64 * 1024 * 1024
SUBLANES = 8
LANES = 128
MXU_WIDTH = 256

NORM_EPS = 1e-6
L2_EPS = 1e-6
SSD_NORM_EPS = 1e-5
RWKV_GN_EPS = 64e-5
LRU_C = 8.0

ROW_TILE = 512
TIME_TILE = 256
GDN_CHUNK = 64
RWKV_CHUNK = 64
SSD_CHUNK = 128
SSD_TIME_TILE = 512
LRU_TIME_TILE = 512
SSD_GROUPS = 4
CONV_TAPS = 4
WEIGHT_PIECES = 8


def _params(n_axes):
    return pltpu.CompilerParams(
        dimension_semantics=("arbitrary",) * n_axes,
        vmem_limit_bytes=V7X_VMEM_BYTES // 8 * 7,
    )


def _resident(shape):
    zeros = (0,) * len(shape)
    return pl.BlockSpec(shape, lambda *_: zeros, pipeline_mode=pl.Buffered(1))


def _rms(x, g, eps):
    return x * lax.rsqrt(jnp.mean(x * x, axis=-1, keepdims=True) + eps) * g


def _sigmoid(x):
    return 1.0 / (1.0 + jnp.exp(-x))


def _silu(x):
    return x * _sigmoid(x)


def _softplus(x):
    return jnp.maximum(x, 0.0) + jnp.log1p(jnp.exp(-jnp.abs(x)))


def _gelu_tanh(x):
    return 0.5 * x * (1.0 + jnp.tanh(0.7978845608028654 * (x + 0.044715 * (x * x * x))))


def _dot(a, b):
    return jnp.dot(a.astype(BF16), b.astype(BF16), preferred_element_type=F32)


def _dot_nt(a, b):
    return lax.dot_general(a.astype(BF16), b.astype(BF16), (((1,), (1,)), ((), ())),
                           preferred_element_type=F32)


def _dot_tn(a, b):
    return lax.dot_general(a.astype(BF16), b.astype(BF16), (((0,), (0,)), ((), ())),
                           preferred_element_type=F32)


def _seg_cumsum(x, axis, seg):
    pos = lax.broadcasted_iota(jnp.int32, x.shape, axis) & (seg - 1)
    s = 1
    while s < seg:
        x = x + jnp.where(pos >= s, pltpu.roll(x, s, axis), 0.0)
        s *= 2
    return x


def _causal_conv(pre, tail_ref, w):
    tt = pre.shape[0]
    ext = jnp.concatenate([tail_ref[...], pre], axis=0)
    y = pre * w[CONV_TAPS - 1:CONV_TAPS, :]
    for k in range(1, CONV_TAPS):
        y = y + pltpu.roll(ext, k, 0)[SUBLANES:, :] * w[CONV_TAPS - 1 - k:CONV_TAPS - k, :]
    tail_ref[...] = pre[tt - SUBLANES:, :]
    return y


def _tri_inv(lows, block):
    n = lows[0].shape[0]
    n_sys = n // block
    row = lax.broadcasted_iota(jnp.int32, (n, n), 0)
    col = lax.broadcasted_iota(jnp.int32, (n, n), 1)
    on_diag = (row >> (block.bit_length() - 1)) == (col >> (block.bit_length() - 1))

    def to_diag(m_l):
        return jnp.where(on_diag, jnp.concatenate([m_l] * n_sys, axis=0), 0.0)

    row_l = lax.broadcasted_iota(jnp.int32, (block, n), 0)
    col_l = lax.broadcasted_iota(jnp.int32, (block, n), 1) & (block - 1)
    eye_l = jnp.where(row_l == col_l, 1.0, 0.0)
    pair_l = (row_l >> 1) == (col_l >> 1)
    t_ls = []
    for low in lows:
        low_l = low[:block]
        for s in range(1, n_sys):
            low_l = low_l + low[s * block:(s + 1) * block]
        t_ls.append(eye_l - jnp.where(pair_l, low_l, 0.0))
    b, sh = 2, 1
    while b < block:
        sub = ((row >> (sh + 1)) == (col >> (sh + 1))) & (((row >> sh) & 1) == 1) & (((col >> sh) & 1) == 0)
        xs = [_dot(t_l, jnp.where(sub, low, 0.0)) for t_l, low in zip(t_ls, lows)]
        t_ls = [t_l - _dot(x, to_diag(t_l)) for t_l, x in zip(t_ls, xs)]
        b, sh = 2 * b, sh + 1
    return [to_diag(t_l) for t_l in t_ls]


def _ffn_half_step(x, g0, g1, win_ref, wout_ref):
    d_ff = wout_ref.shape[0]
    mid = (d_ff // MXU_WIDTH // 2) * MXU_WIDTH
    xn = _rms(x, g0, NORM_EPS).astype(BF16)
    acc = None
    for lo, hi in ((0, mid), (mid, d_ff)) if 0 < mid < d_ff else ((0, d_ff),):
        gate = jnp.dot(xn, win_ref[:, lo:hi], preferred_element_type=F32)
        up = jnp.dot(xn, win_ref[:, d_ff + lo:d_ff + hi], preferred_element_type=F32)
        act = (_silu(gate) * up).astype(BF16)
        part = jnp.dot(act, wout_ref[lo:hi, :], preferred_element_type=F32)
        acc = part if acc is None else acc + part
    return x + _rms(acc, 0.5 * g1, NORM_EPS)


def _stage_weights(streams, idx):
    def copy(s, c):
        src_hbm, _, stage_ref, sem_ref = streams[s]
        rows = stage_ref.shape[1]
        piece = src_hbm.at[idx[0], idx[1], pl.ds(c * rows, rows)]
        return pltpu.make_async_copy(piece, stage_ref.at[c % 2], sem_ref.at[c % 2])

    for s in range(len(streams)):
        copy(s, 0).start(priority=s % 2)
    for c in range(WEIGHT_PIECES):
        for s, (_, dst_ref, stage_ref, _) in enumerate(streams):
            rows = stage_ref.shape[1]
            if c + 1 < WEIGHT_PIECES:
                copy(s, c + 1).start(priority=s % 2)
            copy(s, c).wait()
            dst_ref[c * rows:(c + 1) * rows, :] = stage_ref[c % 2].astype(BF16)


def _ffn_weight_scratch(w_in, w_out):
    (d, f2), (f, _) = w_in.shape[2:], w_out.shape[2:]
    return [pltpu.VMEM((d, f2), BF16), pltpu.VMEM((f, d), BF16),
            pltpu.VMEM((2, d // WEIGHT_PIECES, f2), F32), pltpu.VMEM((2, f // WEIGHT_PIECES, d), F32),
            pltpu.SemaphoreType.DMA((2,)), pltpu.SemaphoreType.DMA((2,))]


def _load_ffn_weights(win_hbm, wout_hbm, idx, win_ref, wout_ref, sin_ref, sout_ref, semi_ref, semo_ref):
    @pl.when(pl.program_id(0) == 0)
    def _load():
        _stage_weights([(win_hbm, win_ref, sin_ref, semi_ref), (wout_hbm, wout_ref, sout_ref, semo_ref)], idx)


def _ffn_kernel(x_ref, g0_ref, g1_ref, win_hbm, wout_hbm, o_ref, win_ref, wout_ref, *stage, idx):
    _load_ffn_weights(win_hbm, wout_hbm, idx, win_ref, wout_ref, *stage)
    o_ref[...] = _ffn_half_step(x_ref[...], g0_ref[...], g1_ref[...], win_ref, wout_ref)


def _ffn(x2d, g0, g1, w_in, w_out, idx):
    n, d = x2d.shape
    row = pl.BlockSpec((ROW_TILE, d), lambda i: (i, 0))
    hbm = pl.BlockSpec(memory_space=pl.ANY)
    return pl.pallas_call(
        functools.partial(_ffn_kernel, idx=idx),
        grid=(n // ROW_TILE,),
        in_specs=[row, _resident((1, d)), _resident((1, d)), hbm, hbm],
        out_specs=row,
        out_shape=jax.ShapeDtypeStruct((n, d), F32),
        scratch_shapes=_ffn_weight_scratch(w_in, w_out),
        compiler_params=_params(1),
        name="ffn",
    )(x2d, g0.reshape(1, d), g1.reshape(1, d), w_in, w_out)


def _mix_out_ffn_kernel(y_ref, wo_ref, go_ref, x_ref, g0_ref, g1_ref, win_hbm, wout_hbm, o_ref, win_ref, wout_ref,
                        *stage, idx):
    _load_ffn_weights(win_hbm, wout_hbm, idx, win_ref, wout_ref, *stage)
    h = jnp.dot(y_ref[...], wo_ref[...], preferred_element_type=F32)
    x = x_ref[...] + _rms(h, go_ref[...], NORM_EPS)
    o_ref[...] = _ffn_half_step(x, g0_ref[...], g1_ref[...], win_ref, wout_ref)


def _mix_out_ffn(y2d, w_o, g_o, x2d, g0, g1, w_in, w_out, idx):
    n, d = x2d.shape
    k = y2d.shape[1]
    row = pl.BlockSpec((ROW_TILE, d), lambda i: (i, 0))
    vec = _resident((1, d))
    hbm = pl.BlockSpec(memory_space=pl.ANY)
    return pl.pallas_call(
        functools.partial(_mix_out_ffn_kernel, idx=idx),
        grid=(n // ROW_TILE,),
        in_specs=[pl.BlockSpec((ROW_TILE, k), lambda i: (i, 0)), _resident(w_o.shape), vec, row, vec, vec, hbm, hbm],
        out_specs=row,
        out_shape=jax.ShapeDtypeStruct((n, d), F32),
        scratch_shapes=_ffn_weight_scratch(w_in, w_out),
        compiler_params=_params(1),
        name="mix_out_ffn",
    )(y2d, w_o, g_o.reshape(1, d), x2d, g0.reshape(1, d), g1.reshape(1, d), w_in, w_out)


def _gdn_chunk_local(q, k, v, beta, gc_col, gc_row, causal, strict):
    decay = jnp.exp(jnp.where(causal, gc_col - gc_row, -jnp.inf))
    lower = jnp.where(strict, _dot_nt(k, k) * decay * beta, 0.0)
    eg = jnp.exp(gc_col)
    rhs = jnp.concatenate([v * beta, k * (beta * eg)], axis=1)
    return lower, rhs, _dot_nt(q, k) * decay, q * eg


def _gdn_kernel(x_ref, gn_ref, w_ref, wbat_ref, cw_ref, alc_ref, dtc_ref, alr_ref, dtr_ref, nw_ref, y_ref,
                tail_ref, s_ref, *, heads, dk, dv, chunk):
    @pl.when(pl.program_id(1) == 0)
    def _init():
        tail_ref[...] = jnp.zeros_like(tail_ref)
        s_ref[...] = jnp.zeros_like(s_ref)

    n_qkv = cw_ref.shape[1]
    o_k, o_v, o_ba = heads * dk, 2 * heads * dk, n_qkv + heads * dv
    xn = _rms(x_ref[0], gn_ref[...], NORM_EPS).astype(BF16)
    qkv = _silu(_causal_conv(jnp.dot(xn, w_ref[:, :n_qkv], preferred_element_type=F32), tail_ref, cw_ref[...]))
    z = jnp.dot(xn, w_ref[:, n_qkv:o_ba], preferred_element_type=F32)
    ba_c = jnp.dot(xn, w_ref[:, o_ba:], preferred_element_type=F32)
    ba_r = _dot_nt(wbat_ref[...], xn)
    beta_all = _sigmoid(ba_c)
    gc_cols = _seg_cumsum(-jnp.exp(alc_ref[...]) * _softplus(ba_c + dtc_ref[...]), 0, chunk)
    gc_rows = _seg_cumsum(-jnp.exp(alr_ref[...]) * _softplus(ba_r[heads:, :] + dtr_ref[...]), 1, chunk)
    tt = qkv.shape[0]
    shift = chunk.bit_length() - 1
    row = lax.broadcasted_iota(jnp.int32, (tt, tt), 0)
    col = lax.broadcasted_iota(jnp.int32, (tt, tt), 1)
    same_chunk = (row >> shift) == (col >> shift)
    causal = same_chunk & (row >= col)
    strict = same_chunk & (row > col)
    hds = []
    for h in range(heads):
        qh = qkv[:, h * dk:(h + 1) * dk]
        kh = qkv[:, o_k + h * dk:o_k + (h + 1) * dk]
        vh = qkv[:, o_v + h * dv:o_v + (h + 1) * dv]
        qh = qh * (lax.rsqrt(jnp.sum(qh * qh, axis=-1, keepdims=True) + L2_EPS) * dk ** -0.5)
        kh = kh * lax.rsqrt(jnp.sum(kh * kh, axis=-1, keepdims=True) + L2_EPS)
        beta = beta_all[:, h:h + 1]
        gc_col = gc_cols[:, heads + h:heads + h + 1]
        gc_row = gc_rows[h:h + 1, :]
        lower, rhs, qk, q_dec = _gdn_chunk_local(qh, kh, vh, beta, gc_col, gc_row, causal, strict)
        hds.append(dict(lower=lower, rhs=rhs, qk=qk, q_dec=q_dec, k=kh, gc=gc_col, state=s_ref[h],
                        v_new=[], o_state=[]))
    for hd, t_inv in zip(hds, _tri_inv([hd["lower"] for hd in hds], chunk)):
        sol = _dot(t_inv, hd["rhs"])
        hd.update(u=sol[:, :dv], w=sol[:, dv:])

    for c in range(tt // chunk):
        sl = slice(c * chunk, (c + 1) * chunk)
        for hd in hds:
            g_last = hd["gc"][(c + 1) * chunk - 1:(c + 1) * chunk, :]
            vn = hd["u"][sl] - _dot(hd["w"][sl], hd["state"])
            hd["o_state"].append(_dot(hd["q_dec"][sl], hd["state"]))
            hd["state"] = hd["state"] * jnp.exp(g_last) + _dot_tn(hd["k"][sl] * jnp.exp(g_last - hd["gc"][sl]), vn)
            hd["v_new"].append(vn)

    for h, hd in enumerate(hds):
        s_ref[h] = hd["state"]
        o = jnp.concatenate(hd["o_state"], axis=0) + _dot(hd["qk"], jnp.concatenate(hd["v_new"], axis=0))
        o = _rms(o, nw_ref[...], NORM_EPS) * _silu(z[:, h * dv:(h + 1) * dv])
        y_ref[0, :, h * dv:(h + 1) * dv] = o.astype(y_ref.dtype)


def _gdn_mixer(x2d, g_in, shape, w_in, conv_w, a_log, dt_bias, norm_w):
    bsz, t, d = shape
    heads = a_log.shape[0]
    dv = norm_w.shape[0]
    n_qkv = conv_w.shape[1]
    dk = (n_qkv - heads * dv) // (2 * heads)
    w = w_in.astype(BF16)
    w_ba_t = w[:, n_qkv + heads * dv:].T
    unused = jnp.zeros_like(a_log)
    tile = pl.BlockSpec((1, TIME_TILE, d), lambda b, i: (b, i, 0))
    y = pl.pallas_call(
        functools.partial(_gdn_kernel, heads=heads, dk=dk, dv=dv, chunk=GDN_CHUNK),
        grid=(bsz, t // TIME_TILE),
        in_specs=[tile, _resident((1, d)), _resident(w.shape), _resident(w_ba_t.shape), _resident(conv_w.shape),
                  _resident((1, 2 * heads)), _resident((1, 2 * heads)), _resident((heads, 1)), _resident((heads, 1)),
                  _resident((1, dv))],
        out_specs=pl.BlockSpec((1, TIME_TILE, heads * dv), lambda b, i: (b, i, 0)),
        out_shape=jax.ShapeDtypeStruct((bsz, t, heads * dv), BF16),
        scratch_shapes=[pltpu.VMEM((SUBLANES, n_qkv), F32), pltpu.VMEM((heads, dk, dv), F32)],
        compiler_params=_params(2),
        name="gdn",
    )(x2d.reshape(bsz, t, d), g_in.reshape(1, d), w, w_ba_t, conv_w,
      jnp.concatenate([unused, a_log]).reshape(1, 2 * heads), jnp.concatenate([unused, dt_bias]).reshape(1, 2 * heads),
      a_log.reshape(heads, 1), dt_bias.reshape(heads, 1), norm_w.reshape(1, dv))
    return y.reshape(bsz * t, heads * dv)


def _rwkv_project(x_ref, g_ref, mu_ref, wrkv_ref, w0_ref, w1_ref, w2_ref, a0_ref, a1_ref, a2_ref,
                  g1_ref, g2_ref, r_ref, k_ref, v_ref, w_ref, a_ref, go_ref, prev_ref):
    hn = _rms(x_ref[0], g_ref[...], NORM_EPS)
    tm = hn.shape[0]
    ext = jnp.concatenate([prev_ref[...], hn], axis=0)
    xx = pltpu.roll(ext, 1, 0)[SUBLANES:, :] - hn
    prev_ref[...] = hn[tm - SUBLANES:, :]
    mu = mu_ref[...]

    def mix(i):
        return (hn + xx * mu[i:i + 1, :]).astype(BF16)

    r_ref[0] = jnp.dot(mix(0), wrkv_ref[0], preferred_element_type=F32)
    k_ref[0] = jnp.dot(mix(1), wrkv_ref[1], preferred_element_type=F32)
    v_ref[0] = jnp.dot(mix(2), wrkv_ref[2], preferred_element_type=F32)
    w_ref[0] = w0_ref[...] + _dot(jnp.tanh(jnp.dot(mix(3), w1_ref[...], preferred_element_type=F32)), w2_ref[...])
    a_ref[0] = a0_ref[...] + _dot(jnp.dot(mix(4), a1_ref[...], preferred_element_type=F32), a2_ref[...])
    go_ref[0] = _dot(_sigmoid(jnp.dot(mix(5), g1_ref[...], preferred_element_type=F32)), g2_ref[...])


def _rwkv_chunk_local(r, k, v, w_pre, a_pre, kk_w, ka_w, *, hd, chunk):
    tt, width = r.shape
    n_heads = width // hd
    lane = lax.broadcasted_iota(jnp.int32, (1, width), 1)
    head_of_lane = [(lane >= j * hd) & (lane < (j + 1) * hd) for j in range(n_heads)]

    def by_head(parts):
        out = parts[-1]
        for m, p in zip(head_of_lane[:-1], parts[:-1]):
            out = jnp.where(m, p, out)
        return out

    log_w = -jnp.exp(-_softplus(-w_pre) - 0.5)
    a = _sigmoid(a_pre)
    kk = k * kk_w
    kk = kk * lax.rsqrt(_head_sum(kk * kk, hd) + L2_EPS)
    k2 = k * (1.0 + (a - 1.0) * ka_w)
    cum = _seg_cumsum(log_w, 0, chunk)
    inv = jnp.exp(-cum)
    a_t = -kk * jnp.exp(cum - log_w)
    b_t = kk * a * inv
    k_t = k2 * inv
    r_t = r * jnp.exp(cum)

    n_chunks = tt // chunk
    lane_l = lax.broadcasted_iota(jnp.int32, (chunk, LANES), 1)
    row_l = lax.broadcasted_iota(jnp.int32, (chunk, LANES), 0)
    col_l = lane_l & (chunk - 1)
    upper_half = lane_l >= chunk
    zeros = jnp.zeros((chunk, LANES), F32)
    v_swapped = jnp.concatenate([v[(c ^ 1) * chunk:((c ^ 1) + 1) * chunk] for c in range(n_chunks)], axis=0)

    def place(block, c):
        return jnp.concatenate([block if col == c // 2 else zeros for col in range(n_chunks // 2)], axis=1)

    lows, rhs, y0_parts, p_rb = [], [], [], []
    for m in head_of_lane:
        a_m, r_m = jnp.where(m, a_t, 0.0), jnp.where(m, r_t, 0.0)
        ab, ak, rb, rk = [], [], [], []
        for c in range(n_chunks):
            sl = slice(c * chunk, (c + 1) * chunk)
            right = [b_t[sl], k_t[sl]] if c % 2 == 0 else [k_t[sl], b_t[sl]]
            g = _dot_nt(jnp.concatenate([a_m[sl], r_m[sl]], axis=0), jnp.concatenate(right, axis=0))
            b_half = upper_half if c % 2 else ~upper_half
            top, bot = g[:chunk], g[chunk:]
            ab.append(place(jnp.where(b_half & (row_l > col_l), -top, 0.0), c))
            ak.append(place(jnp.where(~b_half & (row_l > col_l), top, 0.0), c))
            rb.append(place(jnp.where(b_half & (row_l >= col_l), bot, 0.0), c))
            rk.append(place(jnp.where(~b_half & (row_l >= col_l), bot, 0.0), c))
        lows.append(jnp.concatenate(ab, axis=0))
        p_rb.append(jnp.concatenate(rb, axis=0))
        rhs.append(jnp.concatenate([_dot(jnp.concatenate(ak, axis=0), v_swapped), a_t], axis=1))
        y0_parts.append(_dot(jnp.concatenate(rk, axis=0), v_swapped))
    return dict(lows=lows, rhs=rhs, y0=by_head(y0_parts), p_rb=p_rb, by_head=by_head,
                b_t=b_t, k_t=k_t, r_t=r_t, cum=cum, k2=k2, r=r, v=v)


def _head_sum(x, hd):
    lane = lax.broadcasted_iota(jnp.int32, (1, x.shape[1]), 1)
    tot = jnp.zeros_like(x)
    for j in range(x.shape[1] // hd):
        m = (lane >= j * hd) & (lane < (j + 1) * hd)
        tot = tot + jnp.where(m, jnp.sum(jnp.where(m, x, 0.0), axis=-1, keepdims=True), 0.0)
    return tot


def _rwkv_kernel(x_ref, gn_ref, mu_ref, wrkv_ref, w0_ref, w1_ref, w2_ref, a0_ref, a1_ref, a2_ref, g1_ref, g2_ref,
                 kk_ref, ka_ref, rk_ref, lnw_ref, lnb_ref, y_ref,
                 prev_ref, s_ref, r_ref, k_ref, v_ref, w_ref, a_ref, g_ref, *, hd, chunk):
    @pl.when(pl.program_id(1) == 0)
    def _init():
        prev_ref[...] = jnp.zeros_like(prev_ref)
        s_ref[...] = jnp.zeros_like(s_ref)

    _rwkv_project(x_ref, gn_ref, mu_ref, wrkv_ref, w0_ref, w1_ref, w2_ref, a0_ref, a1_ref, a2_ref, g1_ref, g2_ref,
                  r_ref, k_ref, v_ref, w_ref, a_ref, g_ref, prev_ref)
    _rwkv_recurrence(r_ref, k_ref, v_ref, w_ref, a_ref, g_ref, kk_ref, ka_ref, rk_ref, lnw_ref, lnb_ref,
                     y_ref, s_ref, hd=hd, chunk=chunk)


def _rwkv_recurrence(r_ref, k_ref, v_ref, w_ref, a_ref, g_ref, kk_ref, ka_ref, rk_ref, lnw_ref, lnb_ref,
                     y_ref, s_ref, *, hd, chunk):
    tt, width = r_ref.shape[1:]
    srow = lax.broadcasted_iota(jnp.int32, (LANES, LANES), 0)
    scol = lax.broadcasted_iota(jnp.int32, (LANES, LANES), 1)
    same_head = None
    for j in range(LANES // hd):
        blk = (srow >= j * hd) & (srow < (j + 1) * hd) & (scol >= j * hd) & (scol < (j + 1) * hd)
        same_head = blk if same_head is None else (same_head | blk)

    groups = []
    for grp in range(width // LANES):
        ls = slice(grp * LANES, (grp + 1) * LANES)
        loc = _rwkv_chunk_local(r_ref[0, :, ls], k_ref[0, :, ls], v_ref[0, :, ls], w_ref[0, :, ls], a_ref[0, :, ls],
                                kk_ref[:, ls], ka_ref[:, ls], hd=hd, chunk=chunk)
        loc.update(state=s_ref[grp], us=[], rs=[])
        groups.append(loc)
    t_invs = _tri_inv([low for gp in groups for low in gp["lows"]], chunk)
    for gi, gp in enumerate(groups):
        n_h = len(gp["lows"])
        sols = [_dot(t_inv, rhs) for t_inv, rhs in zip(t_invs[gi * n_h:(gi + 1) * n_h], gp["rhs"])]
        gp.update(u0=gp["by_head"]([s[:, :LANES] for s in sols]), w_t=gp["by_head"]([s[:, LANES:] for s in sols]))

    for c in range(tt // chunk):
        sl = slice(c * chunk, (c + 1) * chunk)
        for gp in groups:
            u_c = gp["u0"][sl] + _dot_nt(gp["w_t"][sl], gp["state"])
            gp["rs"].append(_dot_nt(gp["r_t"][sl], gp["state"]))
            decay_end = jnp.exp(gp["cum"][(c + 1) * chunk - 1:(c + 1) * chunk, :])
            upd = gp["state"] + _dot_tn(u_c, gp["b_t"][sl]) + _dot_tn(gp["v"][sl], gp["k_t"][sl])
            gp["state"] = jnp.where(same_head, upd, 0.0) * decay_end
            gp["us"].append(u_c)

    for grp, gp in enumerate(groups):
        ls = slice(grp * LANES, (grp + 1) * LANES)
        s_ref[grp] = gp["state"]
        u = jnp.concatenate(gp["us"], axis=0)
        y = jnp.concatenate(gp["rs"], axis=0) + gp["y0"] + gp["by_head"]([_dot(p, u) for p in gp["p_rb"]])
        mean = _head_sum(y, hd) * (1.0 / hd)
        yc = y - mean
        var = _head_sum(yc * yc, hd) * (1.0 / hd)
        y = yc * lax.rsqrt(var + RWKV_GN_EPS) * lnw_ref[:, ls] + lnb_ref[:, ls]
        bonus = _head_sum(gp["r"] * gp["k2"] * rk_ref[:, ls], hd) * gp["v"]
        y_ref[0, :, ls] = ((y + bonus) * g_ref[0, :, ls]).astype(y_ref.dtype)


def _rwkv_mixer(x2d, g_in, shape, mu, w_rkv, w0, w1, w2, a0, a1, a2, g1, g2, k_k, k_a, r_k, ln_w, ln_b):
    bsz, t, d = shape
    hd = r_k.shape[1]
    tile = pl.BlockSpec((1, TIME_TILE, d), lambda b, i: (b, i, 0))
    vec = _resident((1, d))
    proj = pltpu.VMEM((1, TIME_TILE, d), F32)
    y = pl.pallas_call(
        functools.partial(_rwkv_kernel, hd=hd, chunk=RWKV_CHUNK),
        grid=(bsz, t // TIME_TILE),
        in_specs=[tile, vec, _resident(mu.shape), _resident(w_rkv.shape), vec, _resident(w1.shape),
                  _resident(w2.shape), vec, _resident(a1.shape), _resident(a2.shape), _resident(g1.shape),
                  _resident(g2.shape)] + [vec] * 5,
        out_specs=tile,
        out_shape=jax.ShapeDtypeStruct((bsz, t, d), BF16),
        scratch_shapes=[pltpu.VMEM((SUBLANES, d), F32), pltpu.VMEM((d // LANES, LANES, LANES), F32)] + [proj] * 6,
        compiler_params=_params(2),
        name="rwkv",
    )(x2d.reshape(bsz, t, d), g_in.reshape(1, d), mu, w_rkv.astype(BF16), w0.reshape(1, d), w1.astype(BF16),
      w2.astype(BF16), a0.reshape(1, d), a1.astype(BF16), a2.astype(BF16), g1.astype(BF16), g2.astype(BF16),
      k_k.reshape(1, d), k_a.reshape(1, d), r_k.reshape(1, d), ln_w.reshape(1, d), ln_b.reshape(1, d))
    return y.reshape(bsz * t, d)


def _ssd_kernel(x_ref, gn_ref, w_ref, wdtt_ref, cw_ref, cb_ref, dtbc_ref, dtbr_ref, alc_ref, alr_ref, dskip_ref, nw_ref,
                y_ref, tail_ref, h_ref, *, groups, gh, hp, ns, chunk):
    @pl.when(pl.program_id(1) == 0)
    def _init():
        tail_ref[...] = jnp.zeros_like(tail_ref)
        h_ref[...] = jnp.zeros_like(h_ref)

    gw = gh * hp
    inner = groups * gw
    n_xbc = cw_ref.shape[1]
    xn = _rms(x_ref[0], gn_ref[...], NORM_EPS).astype(BF16)
    z = jnp.dot(xn, w_ref[:, :inner], preferred_element_type=F32)
    xbc = jnp.dot(xn, w_ref[:, inner:inner + n_xbc], preferred_element_type=F32)
    xbc = _silu(_causal_conv(xbc, tail_ref, cw_ref[...]) + cb_ref[...])
    dt_c = _softplus(jnp.dot(xn, w_ref[:, inner + n_xbc:], preferred_element_type=F32) + dtbc_ref[...])
    dt_r = _softplus(_dot_nt(wdtt_ref[...], xn) + dtbr_ref[...])
    acs_c = _seg_cumsum(dt_c * -jnp.exp(alc_ref[...]), 0, chunk)
    acs_r = _seg_cumsum(dt_r * -jnp.exp(alr_ref[...]), 1, chunk)
    tt = xbc.shape[0]

    n_heads = groups * gh
    head_row = lax.broadcasted_iota(jnp.int32, (n_heads, inner), 0)
    head_of_lane = lax.broadcasted_iota(jnp.int32, (n_heads, inner), 1) >> (hp.bit_length() - 1)
    spread = jnp.where(head_row == head_of_lane, 1.0, 0.0).astype(BF16)

    def expand(cols):
        rows = cols.shape[0]
        hi = cols.astype(BF16)
        rest = cols - hi.astype(F32)
        mid = rest.astype(BF16)
        lo = (rest - mid.astype(F32)).astype(BF16)
        parts = jnp.dot(jnp.concatenate([hi, mid, lo], axis=0), spread, preferred_element_type=F32)
        return parts[:rows] + parts[rows:2 * rows] + parts[2 * rows:]

    rowi = lax.broadcasted_iota(jnp.int32, (chunk, chunk), 0)
    coli = lax.broadcasted_iota(jnp.int32, (chunk, chunk), 1)
    causal = rowi >= coli
    pair_lo = lax.broadcasted_iota(jnp.int32, (1, 2 * hp), 1) < hp

    states = [h_ref[g] for g in range(groups)]
    ys = [[] for _ in range(groups)]
    for c in range(tt // chunk):
        sl = slice(c * chunk, (c + 1) * chunk)
        ac_c = acs_c[sl]
        ac_r = acs_r[:, sl]
        tot = ac_c[chunk - 1:chunk, :]
        dt_x = expand(dt_c[sl])
        in_decay_x = expand(jnp.exp(tot - ac_c))
        out_decay_x = expand(jnp.exp(ac_c))
        for g in range(groups):
            first = g * gh
            gl = slice(g * gw, (g + 1) * gw)
            xc = xbc[sl, gl]
            bc = xbc[sl, inner + g * ns:inner + (g + 1) * ns]
            cc = xbc[sl, inner + groups * ns + g * ns:inner + groups * ns + (g + 1) * ns]
            xdt = xc * dt_x[:, gl]
            cb = _dot_nt(cc, bc)
            y_pairs = []
            for p in range(gh // 2):
                xp = xdt[:, 2 * p * hp:2 * (p + 1) * hp]
                halves = []
                for e in (first + 2 * p, first + 2 * p + 1):
                    seg = jnp.where(causal, ac_c[:, e:e + 1] - ac_r[e:e + 1, :], -jnp.inf)
                    halves.append(_dot(cb * jnp.exp(seg), xp))
                y_pairs.append(jnp.where(pair_lo, halves[0], halves[1]))
            y_diag = jnp.concatenate(y_pairs, axis=1)
            y_off = _dot(cc, states[g]) * out_decay_x[:, gl]
            new = _dot_tn(bc, xdt * in_decay_x[:, gl])
            states[g] = states[g] * out_decay_x[chunk - 1:chunk, gl] + new
            ys[g].append(y_diag + y_off + xc * dskip_ref[:, gl])
    for g in range(groups):
        h_ref[g] = states[g]
        ls = slice(g * gw, (g + 1) * gw)
        y = jnp.concatenate(ys[g], axis=0) * _silu(z[:, ls])
        y_ref[0, :, ls] = _rms(y, nw_ref[:, ls], SSD_NORM_EPS).astype(y_ref.dtype)


def _ssd_mixer(x2d, g_in, shape, w_in, conv_w, conv_b, dt_bias, a_log, d_skip, norm_w):
    bsz, t, d = shape
    n_heads = a_log.shape[0]
    inner = norm_w.shape[0]
    hp = inner // n_heads
    n_xbc = conv_w.shape[1]
    groups = SSD_GROUPS
    ns = (n_xbc - inner) // (2 * groups)
    gh = n_heads // groups
    w = w_in.astype(BF16)
    w_dt_t = w[:, inner + n_xbc:].T
    tile = SSD_TIME_TILE
    y = pl.pallas_call(
        functools.partial(_ssd_kernel, groups=groups, gh=gh, hp=hp, ns=ns, chunk=SSD_CHUNK),
        grid=(bsz, t // tile),
        in_specs=[pl.BlockSpec((1, tile, d), lambda b, i: (b, i, 0)), _resident((1, d)), _resident(w.shape),
                  _resident(w_dt_t.shape), _resident(conv_w.shape), _resident((1, n_xbc)),
                  _resident((1, n_heads)), _resident((n_heads, 1)), _resident((1, n_heads)), _resident((n_heads, 1)),
                  _resident((1, inner)), _resident((1, inner))],
        out_specs=pl.BlockSpec((1, tile, inner), lambda b, i: (b, i, 0)),
        out_shape=jax.ShapeDtypeStruct((bsz, t, inner), BF16),
        scratch_shapes=[pltpu.VMEM((SUBLANES, n_xbc), F32), pltpu.VMEM((groups, ns, inner // groups), F32)],
        compiler_params=_params(2),
        name="ssd",
    )(x2d.reshape(bsz, t, d), g_in.reshape(1, d), w, w_dt_t, conv_w, conv_b.reshape(1, n_xbc),
      dt_bias.reshape(1, n_heads), dt_bias.reshape(n_heads, 1), a_log.reshape(1, n_heads), a_log.reshape(n_heads, 1),
      jnp.repeat(d_skip, hp).reshape(1, inner), norm_w.reshape(1, inner))
    return y.reshape(bsz * t, inner)


def _lru_kernel(x_ref, gn_ref, w_ref, cw_ref, cb_ref, wg_ref, bg_ref, lam_ref, y_ref, tail_ref, h_ref, a_ref, b_ref):
    @pl.when(pl.program_id(1) == 0)
    def _init():
        tail_ref[...] = jnp.zeros_like(tail_ref)
        h_ref[...] = jnp.zeros_like(h_ref)

    width = lam_ref.shape[1]
    xn = _rms(x_ref[0], gn_ref[...], NORM_EPS).astype(BF16)
    gate = jnp.dot(xn, w_ref[:, :width], preferred_element_type=F32)
    u = _causal_conv(jnp.dot(xn, w_ref[:, width:], preferred_element_type=F32), tail_ref, cw_ref[...]) + cb_ref[...]
    tt = u.shape[0]
    blocks, bw, _ = wg_ref.shape
    groups = (tt // SUBLANES, SUBLANES, bw)
    pos = lax.broadcasted_iota(jnp.int32, groups, 1)
    for n in range(blocks):
        ls = slice(n * bw, (n + 1) * bw)
        un = u[:, ls]
        gates = _dot(un, wg_ref[n])
        r = _sigmoid(gates[:, :bw] + bg_ref[0:1, ls])
        i = _sigmoid(gates[:, bw:] + bg_ref[1:2, ls])
        log_a = -LRU_C * r * _softplus(-lam_ref[:, ls])
        a = jnp.exp(log_a)
        b = jnp.sqrt(1.0 - a * a) * (i * un)
        a, b = a.reshape(groups), b.reshape(groups)
        s = 1
        while s < SUBLANES:
            keep = pos >= s
            b = jnp.where(keep, a * pltpu.roll(b, s, 1) + b, b)
            a = jnp.where(keep, a * pltpu.roll(a, s, 1), a)
            s *= 2
        a_ref[:, ls] = a.reshape(tt, bw)
        b_ref[:, ls] = b.reshape(tt, bw)

    def carry(g, h):
        rows = pl.ds(pl.multiple_of(g * SUBLANES, SUBLANES), SUBLANES)
        hg = b_ref[rows, :] + a_ref[rows, :] * h
        b_ref[rows, :] = hg
        return hg[SUBLANES - 1:SUBLANES, :]

    h_ref[...] = lax.fori_loop(0, tt // SUBLANES, carry, h_ref[...], unroll=8)
    y_ref[0] = (_gelu_tanh(gate) * b_ref[...]).astype(y_ref.dtype)


def _lru_mixer(x2d, g_in, shape, w_in, conv_w, conv_b, w_gates, b_gates, lam):
    bsz, t, d = shape
    width = lam.shape[0]
    tt = LRU_TIME_TILE
    y = pl.pallas_call(
        _lru_kernel,
        grid=(bsz, t // tt),
        in_specs=[pl.BlockSpec((1, tt, d), lambda b, i: (b, i, 0)), _resident((1, d)), _resident(w_in.shape),
                  _resident(conv_w.shape), _resident((1, width)), _resident(w_gates.shape), _resident(b_gates.shape),
                  _resident((1, width))],
        out_specs=pl.BlockSpec((1, tt, width), lambda b, i: (b, i, 0)),
        out_shape=jax.ShapeDtypeStruct((bsz, t, width), BF16),
        scratch_shapes=[pltpu.VMEM((SUBLANES, width), F32), pltpu.VMEM((1, width), F32),
                        pltpu.VMEM((tt, width), F32), pltpu.VMEM((tt, width), F32)],
        compiler_params=_params(2),
        name="lru",
    )(x2d.reshape(bsz, t, d), g_in.reshape(1, d), w_in.astype(BF16), conv_w, conv_b.reshape(1, width),
      w_gates.astype(BF16), b_gates, lam.reshape(1, width))
    return y.reshape(bsz * t, width)


def kernel(x, norm_g, ffn_w_in, ffn_w_out, gdn_w_in, gdn_conv_w, gdn_a_log, gdn_dt_bias, gdn_norm_w, gdn_w_out, rwkv_mu, rwkv_w_rkv, rwkv_w0, rwkv_w1, rwkv_w2, rwkv_a0, rwkv_a1, rwkv_a2, rwkv_g1, rwkv_g2, rwkv_k_k, rwkv_k_a, rwkv_r_k, rwkv_ln_w, rwkv_ln_b, rwkv_w_out, ssd_w_in, ssd_conv_w, ssd_conv_b, ssd_dt_bias, ssd_a_log, ssd_d, ssd_norm_w, ssd_w_out, lru_w_in, lru_conv_w, lru_conv_b, lru_w_gates, lru_b_gates, lru_lam, lru_w_out):
    shape = x.shape
    bsz, t, d = shape
    depth = norm_g.shape[0]
    n_mixers = 4
    x2d = x.reshape(bsz * t, d)
    for i in range(depth):
        m, j = i % n_mixers, i // n_mixers
        g = norm_g[i]
        x2d = _ffn(x2d, g[0], g[1], ffn_w_in, ffn_w_out, (i, 0))
        if m == 0:
            y = _gdn_mixer(x2d, g[2], shape, gdn_w_in[j], gdn_conv_w[j], gdn_a_log[j], gdn_dt_bias[j], gdn_norm_w[j])
            w_o = gdn_w_out[j]
        elif m == 1:
            y = _rwkv_mixer(x2d, g[2], shape, rwkv_mu[j], rwkv_w_rkv[j], rwkv_w0[j], rwkv_w1[j], rwkv_w2[j],
                            rwkv_a0[j], rwkv_a1[j], rwkv_a2[j], rwkv_g1[j], rwkv_g2[j], rwkv_k_k[j], rwkv_k_a[j],
                            rwkv_r_k[j], rwkv_ln_w[j], rwkv_ln_b[j])
            w_o = rwkv_w_out[j]
        elif m == 2:
            y = _ssd_mixer(x2d, g[2], shape, ssd_w_in[j], ssd_conv_w[j], ssd_conv_b[j], ssd_dt_bias[j],
                           ssd_a_log[j], ssd_d[j], ssd_norm_w[j])
            w_o = ssd_w_out[j]
        else:
            y = _lru_mixer(x2d, g[2], shape, lru_w_in[j], lru_conv_w[j], lru_conv_b[j], lru_w_gates[j],
                           lru_b_gates[j], lru_lam[j])
            w_o = lru_w_out[j]
        x2d = _mix_out_ffn(y, w_o.astype(BF16), g[3], x2d, g[4], g[5], ffn_w_in, ffn_w_out, (i, 1))
    return x2d.reshape(shape)
```
